```python
import jax
import jax.numpy as jnp
from jax import lax
import numpy as np

D_MODEL = 2048
BATCH = 8
SEQ = 2048
DEPTH = 2

CHUNK = 64
MIX_WIDTH = D_MODEL
HGRN_WIDTH = MIX_WIDTH // 2
RWKV_WIDTH = MIX_WIDTH - HGRN_WIDTH
HGRN_HEAD_DIM = 128
HGRN_HEADS = HGRN_WIDTH // HGRN_HEAD_DIM
RWKV_HEAD_DIM = 64
RWKV_HEADS = RWKV_WIDTH // RWKV_HEAD_DIM
DECAY_RANK = max(32, int(round(1.8 * D_MODEL ** 0.5 / 32)) * 32)
ICLR_RANK = max(32, int(round(1.8 * D_MODEL ** 0.5 / 32)) * 32)
VRES_RANK = max(32, int(round(1.3 * D_MODEL ** 0.5 / 32)) * 32)
GATE_RANK = max(32, int(round(0.6 * D_MODEL ** 0.8 / 32)) * 32)
HGRN_COLS = 4 * HGRN_WIDTH
R_OFF = 0
K_OFF = RWKV_WIDTH
V_OFF = 2 * RWKV_WIDTH
WD_OFF = 3 * RWKV_WIDTH
AD_OFF = WD_OFF + DECAY_RANK
GD_OFF = AD_OFF + ICLR_RANK
VD_OFF = GD_OFF + GATE_RANK
RWKV_COLS = VD_OFF
PROJ_BASE = HGRN_COLS + RWKV_COLS
FFN_DIM = ((8 * D_MODEL // 3 + 255) // 256) * 256
N_EXPERTS = 8
TOP_K = 2
EXPERT_DIM = 7 * D_MODEL // 2
N_DENSE = DEPTH - DEPTH // 2
N_MOE = DEPTH // 2
DEEPNORM_ALPHA = (2 * DEPTH) ** 0.25
DEEPNORM_BETA = (8 * DEPTH) ** -0.25
LN_EPS = 1e-5
RMS_EPS = 1e-5
GN_EPS = 64e-5
L2_EPS = 1e-12

kernel_name = 'hgrn2_rwkv7_parallel_moe_deepnorm'


def _layer_norm(x, w, b):
    xf = x.astype(jnp.float32)
    mu = jnp.mean(xf, axis=-1, keepdims=True)
    var = jnp.mean(jnp.square(xf - mu), axis=-1, keepdims=True)
    return ((xf - mu) * lax.rsqrt(var + LN_EPS) * w + b).astype(x.dtype)


def _token_shift(t):
    return jnp.pad(t[:, :-1], ((0, 0), (1, 0), (0, 0)))


def _hgrn2_chunkwise(q, k, v, log_f):
    bsz, seq, nh, dk = q.shape
    dv = v.shape[-1]
    nc = seq // CHUNK

    def to_chunks(t):
        return t.reshape(bsz, nc, CHUNK, nh, t.shape[-1]).transpose(1, 0, 3, 2, 4)

    causal = jnp.tril(jnp.ones((CHUNK, CHUNK), dtype=bool))

    def step(state, inp):
        q_c, k_c, v_c, g_c = inp
        cum = jnp.cumsum(g_c, axis=2)
        diff = cum[:, :, :, None, :] - cum[:, :, None, :, :]
        decay = jnp.exp(jnp.where(causal[:, :, None], diff, -jnp.inf))
        scores = jnp.einsum('bhtk,bhtsk,bhsk->bhts', q_c, decay, k_c)
        o = jnp.einsum('bhts,bhsv->bhtv', scores, v_c) + jnp.einsum('bhtk,bhkv->bhtv', q_c * jnp.exp(cum), state)
        last = cum[:, :, -1:, :]
        state = jnp.exp(last[:, :, 0, :, None]) * state + jnp.einsum('bhsk,bhsv->bhkv', k_c * jnp.exp(last - cum), v_c)
        return state, o

    s0 = jnp.zeros((bsz, nh, dk, dv), jnp.float32)
    _, o = lax.scan(step, s0, (to_chunks(q), to_chunks(k), to_chunks(v), to_chunks(log_f)))
    return o.transpose(1, 0, 3, 2, 4).reshape(bsz, seq, nh, dv)


def _rwkv7_scan(r, w, k, v, a_vec, b_vec):
    bsz, seq, nh, n = r.shape

    def step(state, inp):
        r_t, w_t, k_t, v_t, a_t, b_t = inp
        sa = jnp.einsum('bhvk,bhk->bhv', state, a_t)
        state = state * w_t[:, :, None, :] + sa[..., None] * b_t[:, :, None, :] + v_t[..., None] * k_t[:, :, None, :]
        y = jnp.einsum('bhvk,bhk->bhv', state, r_t)
        return state, y

    s0 = jnp.zeros((bsz, nh, n, n), jnp.float32)
    xs = tuple(jnp.moveaxis(t, 1, 0) for t in (r, w, k, v, a_vec, b_vec))
    _, y = lax.scan(step, s0, xs)
    return jnp.moveaxis(y, 0, 1)


def _rms_norm_swish_gate(o, g, w):
    on = o * lax.rsqrt(jnp.mean(o * o, axis=-1, keepdims=True) + RMS_EPS) * w
    return on * g * jax.nn.sigmoid(g)


def _head_group_norm(y, w, b):
    mu = jnp.mean(y, axis=-1, keepdims=True)
    var = jnp.mean(jnp.square(y - mu), axis=-1, keepdims=True)
    yn = (y - mu) * lax.rsqrt(var + GN_EPS)
    return yn.reshape(y.shape[0], y.shape[1], -1) * w + b


def _swiglu(t, w1, w3, w2):
    return (jax.nn.silu(t @ w1) * (t @ w3)) @ w2


def _moe_swiglu(h, w_router, w1, w3, w2):
    bsz, seq, d = h.shape
    t = h.reshape(-1, d)
    logits = (t @ w_router).astype(jnp.float32)
    top_val, top_idx = lax.top_k(logits, TOP_K)
    top_w = jax.nn.softmax(top_val, axis=-1)
    gates = jnp.sum(jax.nn.one_hot(top_idx, N_EXPERTS, dtype=jnp.float32) * top_w[..., None], axis=1)
    out = jnp.zeros_like(t)
    for e in range(N_EXPERTS):
        out = out + (gates[:, e:e + 1] * _swiglu(t, w1[e], w3[e], w2[e])).astype(t.dtype)
    return out.reshape(bsz, seq, d)


def setup_inputs(seed: int = 0) -> dict:
    key = jax.random.key(seed)
    ks = iter(jax.random.split(key, 48))

    def nrm(shape, scale):
        return jax.random.normal(next(ks), shape, jnp.float32) * scale

    def uni(shape, lo, hi):
        return jax.random.uniform(next(ks), shape, jnp.float32, lo, hi)

    d = D_MODEL
    return {
        'x': nrm((BATCH, SEQ, d), 1.0),
        'w_in': nrm((DEPTH, d, PROJ_BASE), d ** -0.5),
        'w_in_vres': nrm((DEPTH - 1, d, VRES_RANK), d ** -0.5),
        'hgrn_lower_bounds': nrm((DEPTH, HGRN_WIDTH), 0.1),
        'hgrn_norm_w': 1.0 + nrm((DEPTH, HGRN_HEAD_DIM), 0.02),
        'rwkv_mu': uni((DEPTH, RWKV_COLS), 0.0, 1.0),
        'rwkv_mu_vres': uni((DEPTH - 1, VRES_RANK), 0.0, 1.0),
        'rwkv_w0': uni((DEPTH, RWKV_WIDTH), -4.0, -0.5),
        'rwkv_w2': nrm((DEPTH, DECAY_RANK, RWKV_WIDTH), 0.1 * DECAY_RANK ** -0.5),
        'rwkv_a0': nrm((DEPTH, RWKV_WIDTH), 0.1),
        'rwkv_a2': nrm((DEPTH, ICLR_RANK, RWKV_WIDTH), 0.1 * ICLR_RANK ** -0.5),
        'rwkv_v0': 1.0 + nrm((DEPTH - 1, RWKV_WIDTH), 0.1),
        'rwkv_v2': nrm((DEPTH - 1, VRES_RANK, RWKV_WIDTH), 0.1 * VRES_RANK ** -0.5),
        'rwkv_g2': nrm((DEPTH, GATE_RANK, RWKV_WIDTH), GATE_RANK ** -0.5),
        'rwkv_k_k': 0.85 + nrm((DEPTH, RWKV_WIDTH), 0.02),
        'rwkv_k_a': 1.0 + nrm((DEPTH, RWKV_WIDTH), 0.02),
        'rwkv_r_k': nrm((DEPTH, RWKV_HEADS, RWKV_HEAD_DIM), 0.1),
        'rwkv_lnx_w': 1.0 + nrm((DEPTH, RWKV_WIDTH), 0.02),
        'rwkv_lnx_b': nrm((DEPTH, RWKV_WIDTH), 0.02),
        'w_out': nrm((DEPTH, MIX_WIDTH, d), DEEPNORM_BETA * MIX_WIDTH ** -0.5),
        'ln1_w': 1.0 + nrm((DEPTH, d), 0.02),
        'ln1_b': nrm((DEPTH, d), 0.02),
        'ffn_w1': nrm((N_DENSE, d, FFN_DIM), d ** -0.5),
        'ffn_w3': nrm((N_DENSE, d, FFN_DIM), d ** -0.5),
        'ffn_w2': nrm((N_DENSE, FFN_DIM, d), DEEPNORM_BETA * FFN_DIM ** -0.5),
        'moe_router': nrm((N_MOE, d, N_EXPERTS), d ** -0.5),
        'moe_w1': nrm((N_MOE, N_EXPERTS, d, EXPERT_DIM), d ** -0.5),
        'moe_w3': nrm((N_MOE, N_EXPERTS, d, EXPERT_DIM), d ** -0.5),
        'moe_w2': nrm((N_MOE, N_EXPERTS, EXPERT_DIM, d), DEEPNORM_BETA * EXPERT_DIM ** -0.5),
        'ln2_w': 1.0 + nrm((DEPTH, d), 0.02),
        'ln2_b': nrm((DEPTH, d), 0.02),
    }


def reference(x, w_in, w_in_vres, hgrn_lower_bounds, hgrn_norm_w, rwkv_mu, rwkv_mu_vres,
              rwkv_w0, rwkv_w2, rwkv_a0, rwkv_a2, rwkv_v0, rwkv_v2, rwkv_g2,
              rwkv_k_k, rwkv_k_a, rwkv_r_k, rwkv_lnx_w, rwkv_lnx_b, w_out, ln1_w, ln1_b,
              ffn_w1, ffn_w3, ffn_w2, moe_router, moe_w1, moe_w3, moe_w2, ln2_w, ln2_b):
    f32 = jnp.float32
    bsz, seq, _ = x.shape
    lb_all = jax.nn.softmax(hgrn_lower_bounds.astype(f32), axis=0)
    lb_all = jnp.cumsum(lb_all, axis=0) - lb_all[0:1]

    def heads_r(t):
        return t.reshape(bsz, seq, RWKV_HEADS, RWKV_HEAD_DIM)

    v_first = None
    for layer in range(DEPTH):
        if layer == 0:
            w_proj, mu = w_in[0], rwkv_mu[0]
        else:
            w_proj = jnp.concatenate([w_in[layer], w_in_vres[layer - 1]], axis=1)
            mu = jnp.concatenate([rwkv_mu[layer], rwkv_mu_vres[layer - 1]], axis=0)
        p = (x @ w_proj).astype(f32)
        ph, pr = p[..., :HGRN_COLS], p[..., HGRN_COLS:]
        pr = pr + mu.astype(f32) * (_token_shift(pr) - pr)

        hq, hf, hi, hg = [t.reshape(bsz, seq, HGRN_HEADS, HGRN_HEAD_DIM) for t in jnp.split(ph, 4, axis=-1)]
        if layer == 0:
            log_f = jax.nn.log_sigmoid(hf)
            k_in = jax.nn.sigmoid(-hf)
        else:
            lb = lb_all[layer].reshape(HGRN_HEADS, HGRN_HEAD_DIM)
            log_f = jnp.log(lb + (1.0 - lb) * jax.nn.sigmoid(hf))
            k_in = (1.0 - lb) * jax.nn.sigmoid(-hf)
        o_h = _hgrn2_chunkwise(hq, k_in, hi, log_f)
        o_h = _rms_norm_swish_gate(o_h, hg, hgrn_norm_w[layer].astype(f32)).reshape(bsz, seq, HGRN_WIDTH)

        r = pr[..., R_OFF:K_OFF]
        k = pr[..., K_OFF:V_OFF]
        v = pr[..., V_OFF:WD_OFF]
        wd = pr[..., WD_OFF:AD_OFF]
        ad = pr[..., AD_OFF:GD_OFF]
        gd = pr[..., GD_OFF:VD_OFF]
        w_log = -jax.nn.softplus(-(rwkv_w0[layer] + jnp.tanh(wd) @ rwkv_w2[layer])) - 0.5
        decay = jnp.exp(-jnp.exp(w_log))
        iclr = jax.nn.sigmoid(rwkv_a0[layer] + ad @ rwkv_a2[layer])
        gate = jax.nn.sigmoid(gd) @ rwkv_g2[layer]
        if layer == 0:
            v_first = v
        else:
            vd = pr[..., VD_OFF:]
            v = v + (v_first - v) * jax.nn.sigmoid(rwkv_v0[layer - 1] + vd @ rwkv_v2[layer - 1])
        kk = heads_r(k * rwkv_k_k[layer])
        kk = kk / jnp.maximum(jnp.linalg.norm(kk, axis=-1, keepdims=True), L2_EPS)
        k = k * (1.0 + (iclr - 1.0) * rwkv_k_a[layer])
        rh, kh, vh, ah, dh = heads_r(r), heads_r(k), heads_r(v), heads_r(iclr), heads_r(decay)
        y = _rwkv7_scan(rh, dh, kh, vh, -kk, kk * ah)
        bonus = jnp.sum(rh * kh * rwkv_r_k[layer], axis=-1, keepdims=True) * vh
        y = _head_group_norm(y, rwkv_lnx_w[layer], rwkv_lnx_b[layer]) + bonus.reshape(bsz, seq, RWKV_WIDTH)
        o_r = y * gate

        mix = jnp.concatenate([o_h, o_r], axis=-1).astype(x.dtype) @ w_out[layer]
        x = _layer_norm(DEEPNORM_ALPHA * x + mix, ln1_w[layer], ln1_b[layer])

        j = layer // 2
        if layer % 2 == 0:
            ffn = _swiglu(x, ffn_w1[j], ffn_w3[j], ffn_w2[j])
        else:
            ffn = _moe_swiglu(x, moe_router[j], moe_w1[j], moe_w3[j], moe_w2[j])
        x = _layer_norm(DEEPNORM_ALPHA * x + ffn, ln2_w[layer], ln2_b[layer])
    return x
```

```python
import functools

import jax
import jax.numpy as jnp
from jax import lax
from jax.experimental import pallas as pl
from jax.experimental.pallas import tpu as pltpu

LN_EPS = 1e-5
RMS_EPS = 1e-5
GN_EPS = 64e-5
L2_EPS = 1e-12
TOP_K = 2

LANES = 128
VMEM_LIMIT_BYTES = 56 * 1024 * 1024

_F32 = jnp.float32
_BF16 = jnp.bfloat16


def _tile(n, target, align):
    best = None
    for t in range(align, min(n, target) + 1, align):
        if n % t == 0:
            best = t
    return best if best is not None else n


def _params(*sem):
    return pltpu.CompilerParams(dimension_semantics=sem, vmem_limit_bytes=VMEM_LIMIT_BYTES)


def _mm_kernel(x_ref, w_ref, o_ref, wb_ref):
    @pl.when(pl.program_id(1) == 0)
    def _():
        wb_ref[...] = w_ref[...].astype(_BF16)

    o_ref[...] = jnp.dot(x_ref[...], wb_ref[...], preferred_element_type=_F32).astype(o_ref.dtype)


def _matmul(x, w, out_dtype=_F32, tm_target=1024, tn_target=768):
    m, k = x.shape
    n = w.shape[1]
    tm = _tile(m, tm_target, 8)
    tn = _tile(n, tn_target, LANES)
    return pl.pallas_call(
        _mm_kernel,
        grid=(n // tn, m // tm),
        in_specs=[
            pl.BlockSpec((tm, k), lambda j, i: (i, 0)),
            pl.BlockSpec((k, tn), lambda j, i: (0, j)),
        ],
        out_specs=pl.BlockSpec((tm, tn), lambda j, i: (i, j)),
        out_shape=jax.ShapeDtypeStruct((m, n), out_dtype),
        scratch_shapes=[pltpu.VMEM((k, tn), _BF16)],
        compiler_params=_params("arbitrary", "arbitrary"),
        name="matmul",
    )(x, w)


def _up_kernel(*refs, gated):
    if gated:
        x_ref, w1_ref, w3_ref, g_ref, o_ref, w1b_ref, w3b_ref = refs
    else:
        x_ref, w1_ref, w3_ref, o_ref, w1b_ref, w3b_ref = refs

    @pl.when(pl.program_id(1) == 0)
    def _():
        w1b_ref[...] = w1_ref[...].astype(_BF16)
        w3b_ref[...] = w3_ref[...].astype(_BF16)

    x = x_ref[...]
    a = jnp.dot(x, w1b_ref[...], preferred_element_type=_F32)
    b = jnp.dot(x, w3b_ref[...], preferred_element_type=_F32)
    h = a * jax.nn.sigmoid(a) * b
    if gated:
        g = g_ref[...]
        reps = h.shape[1] // g.shape[1]
        h = h * jnp.concatenate([g] * reps, axis=1)
    o_ref[...] = h.astype(o_ref.dtype)


def _swiglu_up(x, w1, w3, gates=None, tm_target=1024, tn_target=512):
    m, k = x.shape
    e, _, f = w1.shape
    tm = _tile(m, tm_target, 8)
    tn = _tile(f, tn_target, LANES)
    npe = f // tn
    w_spec = pl.BlockSpec((None, k, tn), lambda j, i: (j // npe, 0, j % npe))
    in_specs = [pl.BlockSpec((tm, k), lambda j, i: (i, 0)), w_spec, w_spec]
    args = [x, w1, w3]
    if gates is not None:
        in_specs.append(pl.BlockSpec((None, tm, gates.shape[2]), lambda j, i: (j // npe, i, 0)))
        args.append(gates)
    return pl.pallas_call(
        functools.partial(_up_kernel, gated=gates is not None),
        grid=(e * npe, m // tm),
        in_specs=in_specs,
        out_specs=pl.BlockSpec((tm, tn), lambda j, i: (i, j)),
        out_shape=jax.ShapeDtypeStruct((m, e * f), _BF16),
        scratch_shapes=[pltpu.VMEM((k, tn), _BF16), pltpu.VMEM((k, tn), _BF16)],
        compiler_params=_params("arbitrary", "arbitrary"),
        name="swiglu_up",
    )(*args)


def _down_ln_kernel(h_ref, w_ref, res_ref, lw_ref, lb_ref, o_ref, ob_ref, *, alpha, nk):
    kk = pl.program_id(1)
    part = jnp.dot(h_ref[...], w_ref[...].astype(_BF16), preferred_element_type=_F32)

    @pl.when(kk == 0)
    def _():
        o_ref[...] = part

    @pl.when(kk > 0)
    def _():
        o_ref[...] += part

    @pl.when(kk == nk - 1)
    def _():
        y = alpha * res_ref[...] + o_ref[...]
        mu = jnp.mean(y, axis=-1, keepdims=True)
        yc = y - mu
        var = jnp.mean(yc * yc, axis=-1, keepdims=True)
        out = yc * lax.rsqrt(var + LN_EPS) * lw_ref[...] + lb_ref[...]
        o_ref[...] = out
        ob_ref[...] = out.astype(_BF16)


def _down_ln(h, w, res, ln_w, ln_b, alpha, tm_target=512, tk_target=512):
    m = h.shape[0]
    e, f, d = w.shape
    tm = _tile(m, tm_target, 8)
    tk = _tile(f, tk_target, LANES)
    kpe = f // tk
    nk = e * kpe
    return pl.pallas_call(
        functools.partial(_down_ln_kernel, alpha=alpha, nk=nk),
        grid=(m // tm, nk),
        in_specs=[
            pl.BlockSpec((tm, tk), lambda i, kk: (i, kk)),
            pl.BlockSpec((None, tk, d), lambda i, kk: (kk // kpe, kk % kpe, 0)),
            pl.BlockSpec((tm, d), lambda i, kk: (i, 0)),
            pl.BlockSpec((1, d), lambda i, kk: (0, 0)),
            pl.BlockSpec((1, d), lambda i, kk: (0, 0)),
        ],
        out_specs=[
            pl.BlockSpec((tm, d), lambda i, kk: (i, 0)),
            pl.BlockSpec((tm, d), lambda i, kk: (i, 0)),
        ],
        out_shape=[jax.ShapeDtypeStruct((m, d), _F32), jax.ShapeDtypeStruct((m, d), _BF16)],
        compiler_params=_params("arbitrary", "arbitrary"),
        name="down_ln",
    )(h, w, res, ln_w.reshape(1, d), ln_b.reshape(1, d))


def _router_kernel(x_ref, w_ref, o_ref):
    o_ref[...] = jnp.dot(x_ref[...], w_ref[...], preferred_element_type=_F32, precision=lax.Precision.HIGHEST)


def _router_logits(x, w_router, tm_target=512):
    m, d = x.shape
    e = w_router.shape[1]
    tm = _tile(m, tm_target, 8)
    return pl.pallas_call(
        _router_kernel,
        grid=(m // tm,),
        in_specs=[pl.BlockSpec((tm, d), lambda i: (i, 0)), pl.BlockSpec((d, e), lambda i: (0, 0))],
        out_specs=pl.BlockSpec((tm, e), lambda i: (i, 0)),
        out_shape=jax.ShapeDtypeStruct((m, e), _F32),
        compiler_params=_params("arbitrary"),
        name="router_logits",
    )(x, w_router)


def _rwkv_kernel(r_ref, w_ref, k_ref, v_ref, a_ref, b_ref, y_ref, s_ref, *, tt, n, sub):
    @pl.when(pl.program_id(0) == 0)
    def _():
        s_ref[...] = jnp.zeros_like(s_ref)

    nblk = n // sub
    zero = jnp.zeros(v_ref.shape[1:], _F32)

    def step(t, carry):
        def sa_body(j, acc):
            a_blk = a_ref[t, pl.ds(pl.multiple_of(j * sub, sub), sub), :]
            for u in range(sub):
                acc = acc + s_ref[j * sub + u] * a_blk[u:u + 1, :]
            return acc

        sa = lax.fori_loop(0, nblk, sa_body, zero)
        v_t = v_ref[t]

        def upd_body(j, y):
            rows = pl.ds(pl.multiple_of(j * sub, sub), sub)
            w_blk = w_ref[t, rows, :]
            b_blk = b_ref[t, rows, :]
            k_blk = k_ref[t, rows, :]
            r_blk = r_ref[t, rows, :]
            for u in range(sub):
                s_new = s_ref[j * sub + u] * w_blk[u:u + 1, :] + sa * b_blk[u:u + 1, :] + v_t * k_blk[u:u + 1, :]
                s_ref[j * sub + u] = s_new
                y = y + s_new * r_blk[u:u + 1, :]
            return y

        y_ref[t] = lax.fori_loop(0, nblk, upd_body, zero)
        return carry

    lax.fori_loop(0, tt, step, 0)


def _rwkv_scan(r, w, k, v, a, b, tt_target=32):
    s, n, l = r.shape
    tt = _tile(s, tt_target, 1)
    sub = 8 if n % 8 == 0 else 1
    spec = pl.BlockSpec((tt, n, l), lambda i: (i, 0, 0))
    return pl.pallas_call(
        functools.partial(_rwkv_kernel, tt=tt, n=n, sub=sub),
        grid=(s // tt,),
        in_specs=[spec] * 6,
        out_specs=spec,
        out_shape=jax.ShapeDtypeStruct((s, n, l), _F32),
        scratch_shapes=[pltpu.VMEM((n, n, l), _F32)],
        compiler_params=_params("arbitrary"),
        name="rwkv7_scan",
    )(r, w, k, v, a, b)


def _hgrn_kernel(q_ref, f_ref, k_ref, i_ref, o_ref, s_ref, *, tt, dk, sub):
    @pl.when(pl.program_id(0) == 0)
    def _():
        s_ref[...] = jnp.zeros_like(s_ref)

    nblk = dk // sub
    zero = jnp.zeros(i_ref.shape[1:], _F32)

    def step(t, carry):
        i_t = i_ref[t]

        def body(j, o):
            rows = pl.ds(pl.multiple_of(j * sub, sub), sub)
            f_blk = f_ref[t, rows, :]
            k_blk = k_ref[t, rows, :]
            q_blk = q_ref[t, rows, :]
            for u in range(sub):
                s_new = s_ref[j * sub + u] * f_blk[u:u + 1, :] + i_t * k_blk[u:u + 1, :]
                s_ref[j * sub + u] = s_new
                o = o + s_new * q_blk[u:u + 1, :]
            return o

        o_ref[t] = lax.fori_loop(0, nblk, body, zero)
        return carry

    lax.fori_loop(0, tt, step, 0)


def _hgrn_scan(q, f, k, i, tt_target=32):
    s, dk, l = q.shape
    dv = i.shape[1]
    tt = _tile(s, tt_target, 1)
    sub = 8 if dk % 8 == 0 else 1
    kspec = pl.BlockSpec((tt, dk, l), lambda t: (t, 0, 0))
    vspec = pl.BlockSpec((tt, dv, l), lambda t: (t, 0, 0))
    return pl.pallas_call(
        functools.partial(_hgrn_kernel, tt=tt, dk=dk, sub=sub),
        grid=(s // tt,),
        in_specs=[kspec, kspec, kspec, vspec],
        out_specs=vspec,
        out_shape=jax.ShapeDtypeStruct((s, dv, l), _F32),
        scratch_shapes=[pltpu.VMEM((dk, dv, l), _F32)],
        compiler_params=_params("arbitrary"),
        name="hgrn2_scan",
    )(q, f, k, i)


def _to_scan(t, heads):
    bsz, seq, width = t.shape
    n = width // heads
    return t.reshape(bsz, seq, heads, n).transpose(1, 3, 0, 2).reshape(seq, n, bsz * heads)


def _from_scan(t, bsz, heads):
    seq, n, _ = t.shape
    return t.reshape(seq, n, bsz, heads).transpose(2, 0, 3, 1).reshape(bsz, seq, heads * n)


def _hgrn_mix(hq, hf, hi, hg, lb, norm_w, heads):
    bsz, seq, width = hq.shape
    dk = width // heads
    sig = jax.nn.sigmoid(hf)
    f = lb + (1.0 - lb) * sig
    k_in = (1.0 - lb) * jax.nn.sigmoid(-hf)
    halves = 2 if (dk % 2 == 0 and bsz * heads * 2 <= LANES) else 1
    dv_lo = dk // halves

    def key_layout(t):
        t = _to_scan(t, heads)
        return jnp.concatenate([t] * halves, axis=-1)

    i_s = hi.reshape(bsz, seq, heads, halves, dv_lo).transpose(1, 4, 3, 0, 2).reshape(seq, dv_lo, halves * bsz * heads)
    o = _hgrn_scan(key_layout(hq), key_layout(f), key_layout(k_in), i_s)
    o = o.reshape(seq, dv_lo, halves, bsz, heads).transpose(3, 0, 4, 2, 1).reshape(bsz, seq, heads, dk)
    g = hg.reshape(bsz, seq, heads, dk)
    on = o * lax.rsqrt(jnp.mean(o * o, axis=-1, keepdims=True) + RMS_EPS) * norm_w
    return (on * g * jax.nn.sigmoid(g)).reshape(bsz, seq, width)


def kernel(x, w_in, w_in_vres, hgrn_lower_bounds, hgrn_norm_w, rwkv_mu, rwkv_mu_vres, rwkv_w0, rwkv_w2, rwkv_a0, rwkv_a2, rwkv_v0, rwkv_v2, rwkv_g2, rwkv_k_k, rwkv_k_a, rwkv_r_k, rwkv_lnx_w, rwkv_lnx_b, w_out, ln1_w, ln1_b, ffn_w1, ffn_w3, ffn_w2, moe_router, moe_w1, moe_w3, moe_w2, ln2_w, ln2_b):
    bsz, seq, d = x.shape
    depth = w_in.shape[0]
    alpha = (2 * depth) ** 0.25
    hw = hgrn_lower_bounds.shape[1]
    h_heads = hw // hgrn_norm_w.shape[1]
    rw = rwkv_w0.shape[1]
    r_heads, r_n = rwkv_r_k.shape[1], rwkv_r_k.shape[2]
    d_rank, a_rank, g_rank = rwkv_w2.shape[1], rwkv_a2.shape[1], rwkv_g2.shape[1]
    hcols = 4 * hw
    k_off, v_off, wd_off = rw, 2 * rw, 3 * rw
    ad_off = wd_off + d_rank
    gd_off = ad_off + a_rank
    vd_off = gd_off + g_rank
    n_exp = moe_router.shape[2]
    m = bsz * seq

    lb_all = jax.nn.softmax(hgrn_lower_bounds.astype(_F32), axis=0)
    lb_all = jnp.cumsum(lb_all, axis=0) - lb_all[0:1]

    xf = x.reshape(m, d)
    xb = xf.astype(_BF16)
    v_first = None
    for layer in range(depth):
        if layer == 0:
            w_proj, mu = w_in[0], rwkv_mu[0]
        else:
            w_proj = jnp.concatenate([w_in[layer], w_in_vres[layer - 1]], axis=1)
            mu = jnp.concatenate([rwkv_mu[layer], rwkv_mu_vres[layer - 1]], axis=0)
        ncols = w_proj.shape[1]
        pad = (-ncols) % LANES
        if pad:
            w_proj = jnp.pad(w_proj, ((0, 0), (0, pad)))
        p = _matmul(xb, w_proj)[:, :ncols].reshape(bsz, seq, ncols)
        ph, pr = p[..., :hcols], p[..., hcols:]
        pr_prev = jnp.pad(pr[:, :-1], ((0, 0), (1, 0), (0, 0)))
        pr = pr + mu.astype(_F32) * (pr_prev - pr)

        hq, hf, hi, hg = jnp.split(ph, 4, axis=-1)
        o_h = _hgrn_mix(hq, hf, hi, hg, lb_all[layer], hgrn_norm_w[layer].astype(_F32), h_heads)

        r = pr[..., :k_off]
        k = pr[..., k_off:v_off]
        v = pr[..., v_off:wd_off]
        wd = pr[..., wd_off:ad_off]
        ad = pr[..., ad_off:gd_off]
        gd = pr[..., gd_off:vd_off]

        def low_rank(t, w):
            return _matmul(t.reshape(m, t.shape[-1]).astype(_BF16), w).reshape(bsz, seq, w.shape[1])

        w_log = -jax.nn.softplus(-(rwkv_w0[layer] + low_rank(jnp.tanh(wd), rwkv_w2[layer]))) - 0.5
        decay = jnp.exp(-jnp.exp(w_log))
        iclr = jax.nn.sigmoid(rwkv_a0[layer] + low_rank(ad, rwkv_a2[layer]))
        gate = low_rank(jax.nn.sigmoid(gd), rwkv_g2[layer])
        if layer == 0:
            v_first = v
        else:
            vd = pr[..., vd_off:]
            v = v + (v_first - v) * jax.nn.sigmoid(rwkv_v0[layer - 1] + low_rank(vd, rwkv_v2[layer - 1]))

        def heads_r(t):
            return t.reshape(bsz, seq, r_heads, r_n)

        kk = heads_r(k * rwkv_k_k[layer])
        kk = kk / jnp.maximum(jnp.sqrt(jnp.sum(kk * kk, axis=-1, keepdims=True)), L2_EPS)
        kk = kk.reshape(bsz, seq, rw)
        k = k * (1.0 + (iclr - 1.0) * rwkv_k_a[layer])
        y = _rwkv_scan(*[_to_scan(t, r_heads) for t in (r, decay, k, v, -kk, kk * iclr)])
        y = heads_r(_from_scan(y, bsz, r_heads))
        rh, kh, vh = heads_r(r), heads_r(k), heads_r(v)
        bonus = jnp.sum(rh * kh * rwkv_r_k[layer], axis=-1, keepdims=True) * vh
        mu_y = jnp.mean(y, axis=-1, keepdims=True)
        var_y = jnp.mean(jnp.square(y - mu_y), axis=-1, keepdims=True)
        yn = ((y - mu_y) * lax.rsqrt(var_y + GN_EPS)).reshape(bsz, seq, rw)
        y = yn * rwkv_lnx_w[layer] + rwkv_lnx_b[layer] + bonus.reshape(bsz, seq, rw)
        o_r = y * gate

        mix = jnp.concatenate([o_h, o_r], axis=-1).reshape(m, hw + rw).astype(_BF16)
        xf, xb = _down_ln(mix, w_out[layer][None], xf, ln1_w[layer], ln1_b[layer], alpha)

        j = layer // 2
        if layer % 2 == 0:
            h = _swiglu_up(xb, ffn_w1[j][None], ffn_w3[j][None])
            xf, xb = _down_ln(h, ffn_w2[j][None], xf, ln2_w[layer], ln2_b[layer], alpha)
        else:
            logits = _router_logits(xf, moe_router[j])
            top_val, top_idx = lax.top_k(logits, TOP_K)
            top_w = jax.nn.softmax(top_val, axis=-1)
            gates = jnp.sum(jax.nn.one_hot(top_idx, n_exp, dtype=_F32) * top_w[..., None], axis=1)
            gates_b = jnp.broadcast_to(gates.T[:, :, None], (n_exp, m, LANES))
            h = _swiglu_up(xb, moe_w1[j], moe_w3[j], gates_b)
            xf, xb = _down_ln(h, moe_w2[j], xf, ln2_w[layer], ln2_b[layer], alpha)
    return xf.reshape(bsz, seq, d)
```

```python
import functools

import jax
import jax.numpy as jnp
from jax import lax
from jax.experimental import pallas as pl
from jax.experimental.pallas import tpu as pltpu

LN_EPS = 1e-5
RMS_EPS = 1e-5
GN_EPS = 64e-5
L2_EPS = 1e-12
TOP_K = 2

LANES = 128
VMEM_LIMIT_BYTES = 56 * 1024 * 1024

_F32 = jnp.float32
_BF16 = jnp.bfloat16


def _tile(n, target, align):
    best = None
    for t in range(align, min(n, target) + 1, align):
        if n % t == 0:
            best = t
    return best if best is not None else n


def _params(*sem):
    return pltpu.CompilerParams(dimension_semantics=sem, vmem_limit_bytes=VMEM_LIMIT_BYTES)


def _mm_kernel(x_ref, w_ref, o_ref):
    o_ref[...] = jnp.dot(x_ref[...], w_ref[...], preferred_element_type=_F32).astype(o_ref.dtype)


def _matmul(x, w, out_dtype=_F32, tm_target=1024, tn_target=768):
    m, k = x.shape
    n = w.shape[1]
    tm = _tile(m, tm_target, 8)
    tn = _tile(n, tn_target, LANES)
    return pl.pallas_call(
        _mm_kernel,
        grid=(n // tn, m // tm),
        in_specs=[
            pl.BlockSpec((tm, k), lambda j, i: (i, 0)),
            pl.BlockSpec((k, tn), lambda j, i: (0, j)),
        ],
        out_specs=pl.BlockSpec((tm, tn), lambda j, i: (i, j)),
        out_shape=jax.ShapeDtypeStruct((m, n), out_dtype),
        compiler_params=_params("arbitrary", "arbitrary"),
        name="matmul",
    )(x, w)


def _up_kernel(x_ref, w1_ref, w3_ref, o_ref):
    x = x_ref[...]
    a = jnp.dot(x, w1_ref[...], preferred_element_type=_F32)
    b = jnp.dot(x, w3_ref[...], preferred_element_type=_F32)
    o_ref[...] = (a * jax.nn.sigmoid(a) * b).astype(o_ref.dtype)


def _swiglu_up(x, w1, w3, tm_target=1024, tn_target=512):
    m, k = x.shape
    f = w1.shape[1]
    tm = _tile(m, tm_target, 8)
    tn = _tile(f, tn_target, LANES)
    w_spec = pl.BlockSpec((k, tn), lambda j, i: (0, j))
    return pl.pallas_call(
        _up_kernel,
        grid=(f // tn, m // tm),
        in_specs=[pl.BlockSpec((tm, k), lambda j, i: (i, 0)), w_spec, w_spec],
        out_specs=pl.BlockSpec((tm, tn), lambda j, i: (i, j)),
        out_shape=jax.ShapeDtypeStruct((m, f), _BF16),
        compiler_params=_params("arbitrary", "arbitrary"),
        name="swiglu_up",
    )(x, w1, w3)


def _down_ln_kernel(h_ref, w_ref, res_ref, lw_ref, lb_ref, o_ref, ob_ref, *, alpha, nk):
    kk = pl.program_id(1)
    part = jnp.dot(h_ref[...], w_ref[...], preferred_element_type=_F32)

    @pl.when(kk == 0)
    def _():
        o_ref[...] = part

    @pl.when(kk > 0)
    def _():
        o_ref[...] += part

    @pl.when(kk == nk - 1)
    def _():
        y = alpha * res_ref[...] + o_ref[...]
        mu = jnp.mean(y, axis=-1, keepdims=True)
        yc = y - mu
        var = jnp.mean(yc * yc, axis=-1, keepdims=True)
        out = yc * lax.rsqrt(var + LN_EPS) * lw_ref[...] + lb_ref[...]
        o_ref[...] = out
        ob_ref[...] = out.astype(_BF16)


def _down_ln(h, w, res, ln_w, ln_b, alpha, tm_target=512, tk_target=512):
    m = h.shape[0]
    f, d = w.shape
    tm = _tile(m, tm_target, 8)
    tk = _tile(f, tk_target, LANES)
    nk = f // tk
    return pl.pallas_call(
        functools.partial(_down_ln_kernel, alpha=alpha, nk=nk),
        grid=(m // tm, nk),
        in_specs=[
            pl.BlockSpec((tm, tk), lambda i, kk: (i, kk)),
            pl.BlockSpec((tk, d), lambda i, kk: (kk, 0)),
            pl.BlockSpec((tm, d), lambda i, kk: (i, 0)),
            pl.BlockSpec((1, d), lambda i, kk: (0, 0)),
            pl.BlockSpec((1, d), lambda i, kk: (0, 0)),
        ],
        out_specs=[
            pl.BlockSpec((tm, d), lambda i, kk: (i, 0)),
            pl.BlockSpec((tm, d), lambda i, kk: (i, 0)),
        ],
        out_shape=[jax.ShapeDtypeStruct((m, d), _F32), jax.ShapeDtypeStruct((m, d), _BF16)],
        compiler_params=_params("arbitrary", "arbitrary"),
        name="down_ln",
    )(h, w, res, ln_w.reshape(1, d), ln_b.reshape(1, d))


def _router_kernel(x_ref, w_ref, g_ref, e_ref):
    logits = jnp.dot(x_ref[...], w_ref[...], preferred_element_type=_F32, precision=lax.Precision.HIGHEST)
    n_exp = logits.shape[1]
    col = lax.broadcasted_iota(jnp.int32, logits.shape, 1)
    m1 = jnp.max(logits, axis=1, keepdims=True)
    i1 = jnp.min(jnp.where(logits == m1, col, n_exp), axis=1, keepdims=True)
    rest = jnp.where(col == i1, -jnp.inf, logits)
    m2 = jnp.max(rest, axis=1, keepdims=True)
    i2 = jnp.min(jnp.where(rest == m2, col, n_exp), axis=1, keepdims=True)
    ex = jnp.exp(m2 - m1)
    den = 1.0 + ex
    two = lax.broadcasted_iota(jnp.int32, g_ref.shape, 1)
    g_ref[...] = jnp.where(two == 0, 1.0 / den, ex / den)
    e_ref[...] = jnp.where(two == 0, i1, i2)


def _router_top2(x, w_router, tm_target=512):
    m, d = x.shape
    e = w_router.shape[1]
    tm = _tile(m, tm_target, 8)
    return pl.pallas_call(
        _router_kernel,
        grid=(m // tm,),
        in_specs=[pl.BlockSpec((tm, d), lambda i: (i, 0)), pl.BlockSpec((d, e), lambda i: (0, 0))],
        out_specs=[pl.BlockSpec((tm, TOP_K), lambda i: (i, 0)), pl.BlockSpec((tm, TOP_K), lambda i: (i, 0))],
        out_shape=[jax.ShapeDtypeStruct((m, TOP_K), _F32), jax.ShapeDtypeStruct((m, TOP_K), jnp.int32)],
        compiler_params=_params("arbitrary"),
        name="router_top2",
    )(x, w_router)


def _route(top_w, top_e, n_exp, tm):
    t = top_e.shape[0]
    slots = TOP_K * t
    nt = slots // tm + n_exp
    flat_e = top_e.reshape(slots)
    onehot = (flat_e[:, None] == jnp.arange(n_exp, dtype=jnp.int32)[None, :]).astype(jnp.int32)
    counts = jnp.sum(onehot, axis=0)
    rank = jnp.sum((jnp.cumsum(onehot, axis=0) - onehot) * onehot, axis=1)
    padded = ((counts + tm - 1) // tm) * tm
    ends = jnp.cumsum(padded)
    starts = ends - padded
    pos = starts[flat_e] + rank
    order = jnp.argsort(pos).astype(jnp.int32)
    first_slot = jnp.cumsum(counts) - counts
    tile_expert = jnp.minimum(
        jnp.searchsorted(ends, jnp.arange(nt, dtype=jnp.int32) * tm, side="right"), n_exp - 1).astype(jnp.int32)
    row = jnp.arange(nt * tm, dtype=jnp.int32)
    row_e = tile_expert[row // tm]
    row_rank = row - starts[row_e]
    valid = row_rank < counts[row_e]
    slot = order[jnp.clip(first_slot[row_e] + row_rank, 0, slots - 1)]
    src_tok = jnp.where(valid, slot // TOP_K, 0).astype(jnp.int32)
    row_gate = jnp.where(valid, top_w.reshape(slots)[slot], 0.0)
    n_used = (ends[-1] // tm).astype(jnp.int32).reshape(1)
    return src_tok, row_gate, tile_expert, n_used, pos.astype(jnp.int32).reshape(t, TOP_K)


def _gather_kernel(idx_ref, src_ref, o_ref, sem, *, tm):
    def issue(r, carry):
        pltpu.make_async_copy(src_ref.at[pl.ds(idx_ref[0, r], 1)], o_ref.at[pl.ds(r, 1)], sem).start()
        return carry

    lax.fori_loop(0, tm, issue, 0, unroll=8)
    pltpu.make_async_copy(src_ref.at[pl.ds(0, tm)], o_ref, sem).wait()


def _gather_rows(src, idx, tm):
    p = idx.shape[0]
    w = src.shape[1]
    return pl.pallas_call(
        functools.partial(_gather_kernel, tm=tm),
        grid=(p // tm,),
        in_specs=[
            pl.BlockSpec((None, 1, tm), lambda i: (i, 0, 0), memory_space=pltpu.SMEM),
            pl.BlockSpec(memory_space=pl.ANY),
        ],
        out_specs=pl.BlockSpec((tm, w), lambda i: (i, 0)),
        out_shape=jax.ShapeDtypeStruct((p, w), src.dtype),
        scratch_shapes=[pltpu.SemaphoreType.DMA(())],
        compiler_params=_params("arbitrary"),
        name="gather_rows",
    )(idx.reshape(p // tm, 1, tm), src)


def _pack_halves(xb):
    half = xb.shape[1] // 2
    lo = lax.bitcast_convert_type(xb[:, :half], jnp.uint16).astype(jnp.uint32)
    hi = lax.bitcast_convert_type(xb[:, half:], jnp.uint16).astype(jnp.uint32)
    return lo | (hi << 16)


def _unpack_halves(u):
    lo = pltpu.bitcast(u << 16, _F32).astype(_BF16)
    hi = pltpu.bitcast(u & jnp.uint32(0xFFFF0000), _F32).astype(_BF16)
    return jnp.concatenate([lo, hi], axis=1)


def _gup_kernel(te_ref, nu_ref, x_ref, w1_ref, w3_ref, o_ref, w1b_ref, w3b_ref):
    i = pl.program_id(1)
    new_expert = jnp.logical_or(i == 0, te_ref[i] != te_ref[jnp.maximum(i - 1, 0)])

    @pl.when(new_expert)
    def _():
        w1b_ref[...] = w1_ref[...].astype(_BF16)
        w3b_ref[...] = w3_ref[...].astype(_BF16)

    @pl.when(i < nu_ref[0])
    def _():
        x = _unpack_halves(x_ref[...])
        a = jnp.dot(x, w1b_ref[...], preferred_element_type=_F32)
        b = jnp.dot(x, w3b_ref[...], preferred_element_type=_F32)
        o_ref[...] = (a * jax.nn.sigmoid(a) * b).astype(o_ref.dtype)

    @pl.when(i >= nu_ref[0])
    def _():
        o_ref[...] = jnp.zeros_like(o_ref)


def _grouped_up(xs, w1, w3, tile_expert, n_used, tm, tn_target=512):
    p = xs.shape[0]
    _, d, f = w1.shape
    tn = _tile(f, tn_target, LANES)
    w_spec = pl.BlockSpec((None, d, tn), lambda j, i, te, nu: (te[i], 0, j))
    return pl.pallas_call(
        _gup_kernel,
        grid_spec=pltpu.PrefetchScalarGridSpec(
            num_scalar_prefetch=2,
            grid=(f // tn, p // tm),
            in_specs=[pl.BlockSpec((tm, xs.shape[1]), lambda j, i, te, nu: (i, 0)), w_spec, w_spec],
            out_specs=pl.BlockSpec((tm, tn), lambda j, i, te, nu: (i, j)),
            scratch_shapes=[pltpu.VMEM((d, tn), _BF16), pltpu.VMEM((d, tn), _BF16)],
        ),
        out_shape=jax.ShapeDtypeStruct((p, f), _BF16),
        compiler_params=_params("arbitrary", "arbitrary"),
        name="moe_up",
    )(tile_expert, n_used, xs, w1, w3)


def _gdown_kernel(te_ref, nu_ref, h_ref, w_ref, g_ref, o_ref, *, nk):
    i = pl.program_id(0)
    kk = pl.program_id(1)
    used = i < nu_ref[0]

    @pl.when(used)
    def _():
        part = jnp.dot(h_ref[...], w_ref[...].astype(_BF16), preferred_element_type=_F32)

        @pl.when(kk == 0)
        def _():
            o_ref[...] = part

        @pl.when(kk > 0)
        def _():
            o_ref[...] += part

        @pl.when(kk == nk - 1)
        def _():
            g = g_ref[...]
            o_ref[...] = o_ref[...] * jnp.concatenate([g] * (o_ref.shape[1] // g.shape[1]), axis=1)

    @pl.when(jnp.logical_and(jnp.logical_not(used), kk == nk - 1))
    def _():
        o_ref[...] = jnp.zeros_like(o_ref)


def _grouped_down(h, w2, row_gate, tile_expert, n_used, tm, tk_target=512):
    p, f = h.shape
    d = w2.shape[2]
    tk = _tile(f, tk_target, LANES)
    nk = f // tk
    gl = LANES if d % LANES == 0 else d
    gate = jnp.broadcast_to(row_gate[:, None], (p, gl))
    return pl.pallas_call(
        functools.partial(_gdown_kernel, nk=nk),
        grid_spec=pltpu.PrefetchScalarGridSpec(
            num_scalar_prefetch=2,
            grid=(p // tm, nk),
            in_specs=[
                pl.BlockSpec((tm, tk), lambda i, kk, te, nu: (i, kk)),
                pl.BlockSpec((None, tk, d), lambda i, kk, te, nu: (te[i], kk, 0)),
                pl.BlockSpec((tm, gl), lambda i, kk, te, nu: (i, 0)),
            ],
            out_specs=pl.BlockSpec((tm, d), lambda i, kk, te, nu: (i, 0)),
        ),
        out_shape=jax.ShapeDtypeStruct((p, d), _F32),
        compiler_params=_params("arbitrary", "arbitrary"),
        name="moe_down",
    )(tile_expert, n_used, h, w2, gate)


def _combine_ln_kernel(pos_ref, ys_ref, res_ref, lw_ref, lb_ref, o_ref, ob_ref, buf_ref, sem, *, alpha, tc):
    def issue(r, carry):
        for j in range(TOP_K):
            pltpu.make_async_copy(
                ys_ref.at[pl.ds(pos_ref[0, r * TOP_K + j], 1)], buf_ref.at[j, pl.ds(r, 1)], sem.at[j]).start()
        return carry

    lax.fori_loop(0, tc, issue, 0, unroll=4)
    y = alpha * res_ref[...]
    for j in range(TOP_K):
        pltpu.make_async_copy(ys_ref.at[pl.ds(0, tc)], buf_ref.at[j], sem.at[j]).wait()
        y = y + buf_ref[j]
    mu = jnp.mean(y, axis=-1, keepdims=True)
    yc = y - mu
    var = jnp.mean(yc * yc, axis=-1, keepdims=True)
    out = yc * lax.rsqrt(var + LN_EPS) * lw_ref[...] + lb_ref[...]
    o_ref[...] = out
    ob_ref[...] = out.astype(_BF16)


def _combine_ln(ys, pos, res, ln_w, ln_b, alpha, tc_target=256):
    t, d = res.shape
    tc = _tile(t, tc_target, 8)
    return pl.pallas_call(
        functools.partial(_combine_ln_kernel, alpha=alpha, tc=tc),
        grid=(t // tc,),
        in_specs=[
            pl.BlockSpec((None, 1, tc * TOP_K), lambda i: (i, 0, 0), memory_space=pltpu.SMEM),
            pl.BlockSpec(memory_space=pl.ANY),
            pl.BlockSpec((tc, d), lambda i: (i, 0)),
            pl.BlockSpec((1, d), lambda i: (0, 0)),
            pl.BlockSpec((1, d), lambda i: (0, 0)),
        ],
        out_specs=[pl.BlockSpec((tc, d), lambda i: (i, 0)), pl.BlockSpec((tc, d), lambda i: (i, 0))],
        out_shape=[jax.ShapeDtypeStruct((t, d), _F32), jax.ShapeDtypeStruct((t, d), _BF16)],
        scratch_shapes=[pltpu.VMEM((TOP_K, tc, d), _F32), pltpu.SemaphoreType.DMA((TOP_K,))],
        compiler_params=_params("arbitrary"),
        name="combine_ln",
    )(pos.reshape(t // tc, 1, tc * TOP_K), ys, res, ln_w.reshape(1, d), ln_b.reshape(1, d))


def _moe_ln(xf, xb, w_router, w1, w3, w2, ln_w, ln_b, alpha, tm_target=512):
    t = xf.shape[0]
    n_exp = w_router.shape[1]
    tm = _tile(TOP_K * t, tm_target, 8)
    top_w, top_e = _router_top2(xf, w_router)
    src_tok, row_gate, tile_expert, n_used, pos = _route(top_w, top_e, n_exp, tm)
    xs = _gather_rows(_pack_halves(xb), src_tok, tm)
    h = _grouped_up(xs, w1, w3, tile_expert, n_used, tm)
    ys = _grouped_down(h, w2, row_gate, tile_expert, n_used, tm)
    return _combine_ln(ys, pos, xf, ln_w, ln_b, alpha)


def _rwkv_kernel(r_ref, w_ref, k_ref, v_ref, a_ref, b_ref, y_ref, s_ref, *, tt, n, sub):
    @pl.when(pl.program_id(0) == 0)
    def _():
        s_ref[...] = jnp.zeros_like(s_ref)

    nblk = n // sub
    zero = jnp.zeros(v_ref.shape[1:], _F32)

    def step(t, carry):
        def sa_body(j, acc):
            a_blk = a_ref[t, pl.ds(pl.multiple_of(j * sub, sub), sub), :]
            for u in range(sub):
                acc = acc + s_ref[j * sub + u] * a_blk[u:u + 1, :]
            return acc

        sa = lax.fori_loop(0, nblk, sa_body, zero)
        v_t = v_ref[t]

        def upd_body(j, y):
            rows = pl.ds(pl.multiple_of(j * sub, sub), sub)
            w_blk = w_ref[t, rows, :]
            b_blk = b_ref[t, rows, :]
            k_blk = k_ref[t, rows, :]
            r_blk = r_ref[t, rows, :]
            for u in range(sub):
                s_new = s_ref[j * sub + u] * w_blk[u:u + 1, :] + sa * b_blk[u:u + 1, :] + v_t * k_blk[u:u + 1, :]
                s_ref[j * sub + u] = s_new
                y = y + s_new * r_blk[u:u + 1, :]
            return y

        y_ref[t] = lax.fori_loop(0, nblk, upd_body, zero)
        return carry

    lax.fori_loop(0, tt, step, 0)


def _rwkv_scan(r, w, k, v, a, b, tt_target=32):
    s, n, l = r.shape
    tt = _tile(s, tt_target, 1)
    sub = 8 if n % 8 == 0 else 1
    spec = pl.BlockSpec((tt, n, l), lambda i: (i, 0, 0))
    return pl.pallas_call(
        functools.partial(_rwkv_kernel, tt=tt, n=n, sub=sub),
        grid=(s // tt,),
        in_specs=[spec] * 6,
        out_specs=spec,
        out_shape=jax.ShapeDtypeStruct((s, n, l), _F32),
        scratch_shapes=[pltpu.VMEM((n, n, l), _F32)],
        compiler_params=_params("arbitrary"),
        name="rwkv7_scan",
    )(r, w, k, v, a, b)


def _hgrn_kernel(q_ref, f_ref, k_ref, i_ref, o_ref, s_ref, *, tt, dk, sub):
    @pl.when(pl.program_id(0) == 0)
    def _():
        s_ref[...] = jnp.zeros_like(s_ref)

    nblk = dk // sub
    zero = jnp.zeros(i_ref.shape[1:], _F32)

    def step(t, carry):
        i_t = i_ref[t]

        def body(j, o):
            rows = pl.ds(pl.multiple_of(j * sub, sub), sub)
            f_blk = f_ref[t, rows, :]
            k_blk = k_ref[t, rows, :]
            q_blk = q_ref[t, rows, :]
            for u in range(sub):
                s_new = s_ref[j * sub + u] * f_blk[u:u + 1, :] + i_t * k_blk[u:u + 1, :]
                s_ref[j * sub + u] = s_new
                o = o + s_new * q_blk[u:u + 1, :]
            return o

        o_ref[t] = lax.fori_loop(0, nblk, body, zero)
        return carry

    lax.fori_loop(0, tt, step, 0)


def _hgrn_scan(q, f, k, i, tt_target=32):
    s, dk, l = q.shape
    dv = i.shape[1]
    tt = _tile(s, tt_target, 1)
    sub = 8 if dk % 8 == 0 else 1
    kspec = pl.BlockSpec((tt, dk, l), lambda t: (t, 0, 0))
    vspec = pl.BlockSpec((tt, dv, l), lambda t: (t, 0, 0))
    return pl.pallas_call(
        functools.partial(_hgrn_kernel, tt=tt, dk=dk, sub=sub),
        grid=(s // tt,),
        in_specs=[kspec, kspec, kspec, vspec],
        out_specs=vspec,
        out_shape=jax.ShapeDtypeStruct((s, dv, l), _F32),
        scratch_shapes=[pltpu.VMEM((dk, dv, l), _F32)],
        compiler_params=_params("arbitrary"),
        name="hgrn2_scan",
    )(q, f, k, i)


def _to_scan(t, heads):
    bsz, seq, width = t.shape
    n = width // heads
    return t.reshape(bsz, seq, heads, n).transpose(1, 3, 0, 2).reshape(seq, n, bsz * heads)


def _from_scan(t, bsz, heads):
    seq, n, _ = t.shape
    return t.reshape(seq, n, bsz, heads).transpose(2, 0, 3, 1).reshape(bsz, seq, heads * n)


def _hgrn_mix(hq, hf, hi, hg, lb, norm_w, heads):
    bsz, seq, width = hq.shape
    dk = width // heads
    sig = jax.nn.sigmoid(hf)
    f = lb + (1.0 - lb) * sig
    k_in = (1.0 - lb) * jax.nn.sigmoid(-hf)
    halves = 2 if (dk % 2 == 0 and bsz * heads * 2 <= LANES) else 1
    dv_lo = dk // halves

    def key_layout(t):
        t = _to_scan(t, heads)
        return jnp.concatenate([t] * halves, axis=-1)

    i_s = hi.reshape(bsz, seq, heads, halves, dv_lo).transpose(1, 4, 3, 0, 2).reshape(seq, dv_lo, halves * bsz * heads)
    o = _hgrn_scan(key_layout(hq), key_layout(f), key_layout(k_in), i_s)
    o = o.reshape(seq, dv_lo, halves, bsz, heads).transpose(3, 0, 4, 2, 1).reshape(bsz, seq, heads, dk)
    g = hg.reshape(bsz, seq, heads, dk)
    on = o * lax.rsqrt(jnp.mean(o * o, axis=-1, keepdims=True) + RMS_EPS) * norm_w
    return (on * g * jax.nn.sigmoid(g)).reshape(bsz, seq, width)


def kernel(x, w_in, w_in_vres, hgrn_lower_bounds, hgrn_norm_w, rwkv_mu, rwkv_mu_vres, rwkv_w0, rwkv_w2, rwkv_a0, rwkv_a2, rwkv_v0, rwkv_v2, rwkv_g2, rwkv_k_k, rwkv_k_a, rwkv_r_k, rwkv_lnx_w, rwkv_lnx_b, w_out, ln1_w, ln1_b, ffn_w1, ffn_w3, ffn_w2, moe_router, moe_w1, moe_w3, moe_w2, ln2_w, ln2_b):
    bsz, seq, d = x.shape
    depth = w_in.shape[0]
    alpha = (2 * depth) ** 0.25
    hw = hgrn_lower_bounds.shape[1]
    h_heads = hw // hgrn_norm_w.shape[1]
    rw = rwkv_w0.shape[1]
    r_heads, r_n = rwkv_r_k.shape[1], rwkv_r_k.shape[2]
    d_rank, a_rank, g_rank = rwkv_w2.shape[1], rwkv_a2.shape[1], rwkv_g2.shape[1]
    hcols = 4 * hw
    k_off, v_off, wd_off = rw, 2 * rw, 3 * rw
    ad_off = wd_off + d_rank
    gd_off = ad_off + a_rank
    vd_off = gd_off + g_rank
    n_exp = moe_router.shape[2]
    m = bsz * seq

    lb_all = jax.nn.softmax(hgrn_lower_bounds.astype(_F32), axis=0)
    lb_all = jnp.cumsum(lb_all, axis=0) - lb_all[0:1]

    xf = x.reshape(m, d)
    xb = xf.astype(_BF16)
    v_first = None
    for layer in range(depth):
        if layer == 0:
            w_proj, mu = w_in[0], rwkv_mu[0]
        else:
            w_proj = jnp.concatenate([w_in[layer], w_in_vres[layer - 1]], axis=1)
            mu = jnp.concatenate([rwkv_mu[layer], rwkv_mu_vres[layer - 1]], axis=0)
        ncols = w_proj.shape[1]
        pad = (-ncols) % LANES
        if pad:
            w_proj = jnp.pad(w_proj, ((0, 0), (0, pad)))
        p = _matmul(xb, w_proj.astype(_BF16))[:, :ncols].reshape(bsz, seq, ncols)
        ph, pr = p[..., :hcols], p[..., hcols:]
        pr_prev = jnp.pad(pr[:, :-1], ((0, 0), (1, 0), (0, 0)))
        pr = pr + mu.astype(_F32) * (pr_prev - pr)

        hq, hf, hi, hg = jnp.split(ph, 4, axis=-1)
        o_h = _hgrn_mix(hq, hf, hi, hg, lb_all[layer], hgrn_norm_w[layer].astype(_F32), h_heads)

        r = pr[..., :k_off]
        k = pr[..., k_off:v_off]
        v = pr[..., v_off:wd_off]
        wd = pr[..., wd_off:ad_off]
        ad = pr[..., ad_off:gd_off]
        gd = pr[..., gd_off:vd_off]

        def low_rank(t, w):
            return _matmul(t.reshape(m, t.shape[-1]).astype(_BF16), w.astype(_BF16)).reshape(bsz, seq, w.shape[1])

        w_log = -jax.nn.softplus(-(rwkv_w0[layer] + low_rank(jnp.tanh(wd), rwkv_w2[layer]))) - 0.5
        decay = jnp.exp(-jnp.exp(w_log))
        iclr = jax.nn.sigmoid(rwkv_a0[layer] + low_rank(ad, rwkv_a2[layer]))
        gate = low_rank(jax.nn.sigmoid(gd), rwkv_g2[layer])
        if layer == 0:
            v_first = v
        else:
            vd = pr[..., vd_off:]
            v = v + (v_first - v) * jax.nn.sigmoid(rwkv_v0[layer - 1] + low_rank(vd, rwkv_v2[layer - 1]))

        def heads_r(t):
            return t.reshape(bsz, seq, r_heads, r_n)

        kk = heads_r(k * rwkv_k_k[layer])
        kk = kk / jnp.maximum(jnp.sqrt(jnp.sum(kk * kk, axis=-1, keepdims=True)), L2_EPS)
        kk = kk.reshape(bsz, seq, rw)
        k = k * (1.0 + (iclr - 1.0) * rwkv_k_a[layer])
        y = _rwkv_scan(*[_to_scan(t, r_heads) for t in (r, decay, k, v, -kk, kk * iclr)])
        y = heads_r(_from_scan(y, bsz, r_heads))
        rh, kh, vh = heads_r(r), heads_r(k), heads_r(v)
        bonus = jnp.sum(rh * kh * rwkv_r_k[layer], axis=-1, keepdims=True) * vh
        mu_y = jnp.mean(y, axis=-1, keepdims=True)
        var_y = jnp.mean(jnp.square(y - mu_y), axis=-1, keepdims=True)
        yn = ((y - mu_y) * lax.rsqrt(var_y + GN_EPS)).reshape(bsz, seq, rw)
        y = yn * rwkv_lnx_w[layer] + rwkv_lnx_b[layer] + bonus.reshape(bsz, seq, rw)
        o_r = y * gate

        mix = jnp.concatenate([o_h, o_r], axis=-1).reshape(m, hw + rw).astype(_BF16)
        xf, xb = _down_ln(mix, w_out[layer].astype(_BF16), xf, ln1_w[layer], ln1_b[layer], alpha)

        j = layer // 2
        if layer % 2 == 0:
            h = _swiglu_up(xb, ffn_w1[j].astype(_BF16), ffn_w3[j].astype(_BF16))
            xf, xb = _down_ln(h, ffn_w2[j].astype(_BF16), xf, ln2_w[layer], ln2_b[layer], alpha)
        else:
            xf, xb = _moe_ln(xf, xb, moe_router[j], moe_w1[j], moe_w3[j], moe_w2[j], ln2_w[layer], ln2_b[layer], alpha)
    return xf.reshape(bsz, seq, d)
```

```python
import functools

import jax
import jax.numpy as jnp
from jax import lax
from jax.experimental import pallas as pl
from jax.experimental.pallas import tpu as pltpu

LN_EPS = 1e-5
RMS_EPS = 1e-5
GN_EPS = 64e-5
L2_EPS = 1e-12
TOP_K = 2

LANES = 128
VMEM_LIMIT_BYTES = 56 * 1024 * 1024

_F32 = jnp.float32
_BF16 = jnp.bfloat16


def _tile(n, target, align):
    best = None
    for t in range(align, min(n, target) + 1, align):
        if n % t == 0:
            best = t
    return best if best is not None else n


def _params(*sem):
    return pltpu.CompilerParams(dimension_semantics=sem, vmem_limit_bytes=VMEM_LIMIT_BYTES)


def _in_proj_kernel(x_ref, w_ref, mu_ref, o_ref, last_ref, *, shift_from, tiles_per_seq):
    j = pl.program_id(0)
    i = pl.program_id(1)
    acc = jnp.dot(x_ref[...], w_ref[...], preferred_element_type=_F32)

    @pl.when(j < shift_from)
    def _():
        o_ref[...] = acc

    @pl.when(j >= shift_from)
    def _():
        carry = jnp.where(i % tiles_per_seq == 0, 0.0, last_ref[...])
        row = lax.broadcasted_iota(jnp.int32, acc.shape, 0)
        prev = jnp.where(row == 0, carry, pltpu.roll(acc, 1, axis=0))
        last_ref[...] = acc[acc.shape[0] - 1:, :]
        o_ref[...] = acc + mu_ref[...] * (prev - acc)


def _in_proj(x, w, mu, seq, shift_from_col, tm_target=1024, tn_target=512):
    m, k = x.shape
    n = w.shape[1]
    tm = _tile(seq, tm_target, 8)
    tn = _tile(n, tn_target, LANES)
    while shift_from_col % tn:
        tn -= LANES
    return pl.pallas_call(
        functools.partial(_in_proj_kernel, shift_from=shift_from_col // tn, tiles_per_seq=seq // tm),
        grid=(n // tn, m // tm),
        in_specs=[
            pl.BlockSpec((tm, k), lambda j, i: (i, 0)),
            pl.BlockSpec((k, tn), lambda j, i: (0, j)),
            pl.BlockSpec((1, tn), lambda j, i: (0, j)),
        ],
        out_specs=pl.BlockSpec((tm, tn), lambda j, i: (i, j)),
        out_shape=jax.ShapeDtypeStruct((m, n), _F32),
        scratch_shapes=[pltpu.VMEM((1, tn), _F32)],
        compiler_params=_params("arbitrary", "arbitrary"),
        name="in_proj",
    )(x, w, mu.reshape(1, n).astype(_F32))


def _up_kernel(x_ref, w1_ref, w3_ref, o_ref):
    x = x_ref[...]
    a = jnp.dot(x, w1_ref[...], preferred_element_type=_F32)
    b = jnp.dot(x, w3_ref[...], preferred_element_type=_F32)
    o_ref[...] = (a * jax.nn.sigmoid(a) * b).astype(o_ref.dtype)


def _swiglu_up(x, w1, w3, tm_target=1024, tn_target=512):
    m, k = x.shape
    f = w1.shape[1]
    tm = _tile(m, tm_target, 8)
    tn = _tile(f, tn_target, LANES)
    w_spec = pl.BlockSpec((k, tn), lambda j, i: (0, j))
    return pl.pallas_call(
        _up_kernel,
        grid=(f // tn, m // tm),
        in_specs=[pl.BlockSpec((tm, k), lambda j, i: (i, 0)), w_spec, w_spec],
        out_specs=pl.BlockSpec((tm, tn), lambda j, i: (i, j)),
        out_shape=jax.ShapeDtypeStruct((m, f), _BF16),
        compiler_params=_params("arbitrary", "arbitrary"),
        name="swiglu_up",
    )(x, w1, w3)


def _down_ln_kernel(h_ref, w_ref, res_ref, lw_ref, lb_ref, o_ref, ob_ref, *, alpha, nk):
    kk = pl.program_id(1)
    part = jnp.dot(h_ref[...], w_ref[...], preferred_element_type=_F32)

    @pl.when(kk == 0)
    def _():
        o_ref[...] = part

    @pl.when(kk > 0)
    def _():
        o_ref[...] += part

    @pl.when(kk == nk - 1)
    def _():
        y = alpha * res_ref[...] + o_ref[...]
        mu = jnp.mean(y, axis=-1, keepdims=True)
        yc = y - mu
        var = jnp.mean(yc * yc, axis=-1, keepdims=True)
        out = yc * lax.rsqrt(var + LN_EPS) * lw_ref[...] + lb_ref[...]
        o_ref[...] = out
        ob_ref[...] = out.astype(_BF16)


def _down_ln(h, w, res, ln_w, ln_b, alpha, tm_target=512, tk_target=512):
    m = h.shape[0]
    f, d = w.shape
    tm = _tile(m, tm_target, 8)
    tk = _tile(f, tk_target, LANES)
    nk = f // tk
    return pl.pallas_call(
        functools.partial(_down_ln_kernel, alpha=alpha, nk=nk),
        grid=(m // tm, nk),
        in_specs=[
            pl.BlockSpec((tm, tk), lambda i, kk: (i, kk)),
            pl.BlockSpec((tk, d), lambda i, kk: (kk, 0)),
            pl.BlockSpec((tm, d), lambda i, kk: (i, 0)),
            pl.BlockSpec((1, d), lambda i, kk: (0, 0)),
            pl.BlockSpec((1, d), lambda i, kk: (0, 0)),
        ],
        out_specs=[
            pl.BlockSpec((tm, d), lambda i, kk: (i, 0)),
            pl.BlockSpec((tm, d), lambda i, kk: (i, 0)),
        ],
        out_shape=[jax.ShapeDtypeStruct((m, d), _F32), jax.ShapeDtypeStruct((m, d), _BF16)],
        compiler_params=_params("arbitrary", "arbitrary"),
        name="down_ln",
    )(h, w, res, ln_w.reshape(1, d), ln_b.reshape(1, d))


def _router_kernel(x_ref, w_ref, g_ref, e_ref):
    logits = jnp.dot(x_ref[...], w_ref[...], preferred_element_type=_F32, precision=lax.Precision.HIGHEST)
    n_exp = logits.shape[1]
    col = lax.broadcasted_iota(jnp.int32, logits.shape, 1)
    m1 = jnp.max(logits, axis=1, keepdims=True)
    i1 = jnp.min(jnp.where(logits == m1, col, n_exp), axis=1, keepdims=True)
    rest = jnp.where(col == i1, -jnp.inf, logits)
    m2 = jnp.max(rest, axis=1, keepdims=True)
    i2 = jnp.min(jnp.where(rest == m2, col, n_exp), axis=1, keepdims=True)
    ex = jnp.exp(m2 - m1)
    den = 1.0 + ex
    two = lax.broadcasted_iota(jnp.int32, g_ref.shape, 1)
    g_ref[...] = jnp.where(two == 0, 1.0 / den, ex / den)
    e_ref[...] = jnp.where(two == 0, i1, i2)


def _router_top2(x, w_router, tm_target=512):
    m, d = x.shape
    e = w_router.shape[1]
    tm = _tile(m, tm_target, 8)
    return pl.pallas_call(
        _router_kernel,
        grid=(m // tm,),
        in_specs=[pl.BlockSpec((tm, d), lambda i: (i, 0)), pl.BlockSpec((d, e), lambda i: (0, 0))],
        out_specs=[pl.BlockSpec((tm, TOP_K), lambda i: (i, 0)), pl.BlockSpec((tm, TOP_K), lambda i: (i, 0))],
        out_shape=[jax.ShapeDtypeStruct((m, TOP_K), _F32), jax.ShapeDtypeStruct((m, TOP_K), jnp.int32)],
        compiler_params=_params("arbitrary"),
        name="router_top2",
    )(x, w_router)


def _route(top_w, top_e, n_exp, tm):
    t = top_e.shape[0]
    slots = TOP_K * t
    nt = slots // tm + n_exp
    flat_e = top_e.reshape(slots)
    onehot = (flat_e[:, None] == jnp.arange(n_exp, dtype=jnp.int32)[None, :]).astype(jnp.int32)
    counts = jnp.sum(onehot, axis=0)
    rank = jnp.sum((jnp.cumsum(onehot, axis=0) - onehot) * onehot, axis=1)
    padded = ((counts + tm - 1) // tm) * tm
    ends = jnp.cumsum(padded)
    starts = ends - padded
    pos = starts[flat_e] + rank
    order = jnp.argsort(pos).astype(jnp.int32)
    first_slot = jnp.cumsum(counts) - counts
    tile_expert = jnp.minimum(
        jnp.searchsorted(ends, jnp.arange(nt, dtype=jnp.int32) * tm, side="right"), n_exp - 1).astype(jnp.int32)
    row = jnp.arange(nt * tm, dtype=jnp.int32)
    row_e = tile_expert[row // tm]
    row_rank = row - starts[row_e]
    valid = row_rank < counts[row_e]
    slot = order[jnp.clip(first_slot[row_e] + row_rank, 0, slots - 1)]
    src_tok = jnp.where(valid, slot // TOP_K, 0).astype(jnp.int32)
    row_gate = jnp.where(valid, top_w.reshape(slots)[slot], 0.0)
    n_used = (ends[-1] // tm).astype(jnp.int32).reshape(1)
    return src_tok, row_gate, tile_expert, n_used, pos.astype(jnp.int32).reshape(t, TOP_K)


def _gather_kernel(idx_ref, src_ref, o_ref, sem, *, tm):
    def issue(r, carry):
        pltpu.make_async_copy(src_ref.at[pl.ds(idx_ref[0, r], 1)], o_ref.at[pl.ds(r, 1)], sem).start()
        return carry

    lax.fori_loop(0, tm, issue, 0, unroll=8)
    pltpu.make_async_copy(src_ref.at[pl.ds(0, tm)], o_ref, sem).wait()


def _gather_rows(src, idx, tm):
    p = idx.shape[0]
    w = src.shape[1]
    return pl.pallas_call(
        functools.partial(_gather_kernel, tm=tm),
        grid=(p // tm,),
        in_specs=[
            pl.BlockSpec((None, 1, tm), lambda i: (i, 0, 0), memory_space=pltpu.SMEM),
            pl.BlockSpec(memory_space=pl.ANY),
        ],
        out_specs=pl.BlockSpec((tm, w), lambda i: (i, 0)),
        out_shape=jax.ShapeDtypeStruct((p, w), src.dtype),
        scratch_shapes=[pltpu.SemaphoreType.DMA(())],
        compiler_params=_params("arbitrary"),
        name="gather_rows",
    )(idx.reshape(p // tm, 1, tm), src)


def _pack_halves(xb):
    half = xb.shape[1] // 2
    lo = lax.bitcast_convert_type(xb[:, :half], jnp.uint16).astype(jnp.uint32)
    hi = lax.bitcast_convert_type(xb[:, half:], jnp.uint16).astype(jnp.uint32)
    return lo | (hi << 16)


def _unpack_halves(u):
    lo = pltpu.bitcast(u << 16, _F32).astype(_BF16)
    hi = pltpu.bitcast(u & jnp.uint32(0xFFFF0000), _F32).astype(_BF16)
    return jnp.concatenate([lo, hi], axis=1)


def _gup_kernel(te_ref, nu_ref, x_ref, w1_ref, w3_ref, o_ref, w1b_ref, w3b_ref):
    i = pl.program_id(1)
    new_expert = jnp.logical_or(i == 0, te_ref[i] != te_ref[jnp.maximum(i - 1, 0)])

    @pl.when(new_expert)
    def _():
        w1b_ref[...] = w1_ref[...].astype(_BF16)
        w3b_ref[...] = w3_ref[...].astype(_BF16)

    @pl.when(i < nu_ref[0])
    def _():
        x = _unpack_halves(x_ref[...])
        a = jnp.dot(x, w1b_ref[...], preferred_element_type=_F32)
        b = jnp.dot(x, w3b_ref[...], preferred_element_type=_F32)
        o_ref[...] = (a * jax.nn.sigmoid(a) * b).astype(o_ref.dtype)

    @pl.when(i >= nu_ref[0])
    def _():
        o_ref[...] = jnp.zeros_like(o_ref)


def _grouped_up(xs, w1, w3, tile_expert, n_used, tm, tn_target=512):
    p = xs.shape[0]
    _, d, f = w1.shape
    tn = _tile(f, tn_target, LANES)
    w_spec = pl.BlockSpec((None, d, tn), lambda j, i, te, nu: (te[i], 0, j))
    return pl.pallas_call(
        _gup_kernel,
        grid_spec=pltpu.PrefetchScalarGridSpec(
            num_scalar_prefetch=2,
            grid=(f // tn, p // tm),
            in_specs=[pl.BlockSpec((tm, xs.shape[1]), lambda j, i, te, nu: (i, 0)), w_spec, w_spec],
            out_specs=pl.BlockSpec((tm, tn), lambda j, i, te, nu: (i, j)),
            scratch_shapes=[pltpu.VMEM((d, tn), _BF16), pltpu.VMEM((d, tn), _BF16)],
        ),
        out_shape=jax.ShapeDtypeStruct((p, f), _BF16),
        compiler_params=_params("arbitrary", "arbitrary"),
        name="moe_up",
    )(tile_expert, n_used, xs, w1, w3)


def _gdown_kernel(te_ref, nu_ref, h_ref, w_ref, g_ref, o_ref, *, nk):
    i = pl.program_id(0)
    kk = pl.program_id(1)
    used = i < nu_ref[0]

    @pl.when(used)
    def _():
        part = jnp.dot(h_ref[...], w_ref[...].astype(_BF16), preferred_element_type=_F32)

        @pl.when(kk == 0)
        def _():
            o_ref[...] = part

        @pl.when(kk > 0)
        def _():
            o_ref[...] += part

        @pl.when(kk == nk - 1)
        def _():
            g = g_ref[...]
            o_ref[...] = o_ref[...] * jnp.concatenate([g] * (o_ref.shape[1] // g.shape[1]), axis=1)

    @pl.when(jnp.logical_and(jnp.logical_not(used), kk == nk - 1))
    def _():
        o_ref[...] = jnp.zeros_like(o_ref)


def _grouped_down(h, w2, row_gate, tile_expert, n_used, tm, tk_target=512):
    p, f = h.shape
    d = w2.shape[2]
    tk = _tile(f, tk_target, LANES)
    nk = f // tk
    gl = LANES if d % LANES == 0 else d
    gate = jnp.broadcast_to(row_gate[:, None], (p, gl))
    return pl.pallas_call(
        functools.partial(_gdown_kernel, nk=nk),
        grid_spec=pltpu.PrefetchScalarGridSpec(
            num_scalar_prefetch=2,
            grid=(p // tm, nk),
            in_specs=[
                pl.BlockSpec((tm, tk), lambda i, kk, te, nu: (i, kk)),
                pl.BlockSpec((None, tk, d), lambda i, kk, te, nu: (te[i], kk, 0)),
                pl.BlockSpec((tm, gl), lambda i, kk, te, nu: (i, 0)),
            ],
            out_specs=pl.BlockSpec((tm, d), lambda i, kk, te, nu: (i, 0)),
        ),
        out_shape=jax.ShapeDtypeStruct((p, d), _F32),
        compiler_params=_params("arbitrary", "arbitrary"),
        name="moe_down",
    )(tile_expert, n_used, h, w2, gate)


def _combine_ln_kernel(pos_ref, ys_ref, res_ref, lw_ref, lb_ref, o_ref, ob_ref, buf_ref, sem, *, alpha, tc):
    def issue(r, carry):
        for j in range(TOP_K):
            pltpu.make_async_copy(
                ys_ref.at[pl.ds(pos_ref[0, r * TOP_K + j], 1)], buf_ref.at[j, pl.ds(r, 1)], sem.at[j]).start()
        return carry

    lax.fori_loop(0, tc, issue, 0, unroll=4)
    y = alpha * res_ref[...]
    for j in range(TOP_K):
        pltpu.make_async_copy(ys_ref.at[pl.ds(0, tc)], buf_ref.at[j], sem.at[j]).wait()
        y = y + buf_ref[j]
    mu = jnp.mean(y, axis=-1, keepdims=True)
    yc = y - mu
    var = jnp.mean(yc * yc, axis=-1, keepdims=True)
    out = yc * lax.rsqrt(var + LN_EPS) * lw_ref[...] + lb_ref[...]
    o_ref[...] = out
    ob_ref[...] = out.astype(_BF16)


def _combine_ln(ys, pos, res, ln_w, ln_b, alpha, tc_target=256):
    t, d = res.shape
    tc = _tile(t, tc_target, 8)
    return pl.pallas_call(
        functools.partial(_combine_ln_kernel, alpha=alpha, tc=tc),
        grid=(t // tc,),
        in_specs=[
            pl.BlockSpec((None, 1, tc * TOP_K), lambda i: (i, 0, 0), memory_space=pltpu.SMEM),
            pl.BlockSpec(memory_space=pl.ANY),
            pl.BlockSpec((tc, d), lambda i: (i, 0)),
            pl.BlockSpec((1, d), lambda i: (0, 0)),
            pl.BlockSpec((1, d), lambda i: (0, 0)),
        ],
        out_specs=[pl.BlockSpec((tc, d), lambda i: (i, 0)), pl.BlockSpec((tc, d), lambda i: (i, 0))],
        out_shape=[jax.ShapeDtypeStruct((t, d), _F32), jax.ShapeDtypeStruct((t, d), _BF16)],
        scratch_shapes=[pltpu.VMEM((TOP_K, tc, d), _F32), pltpu.SemaphoreType.DMA((TOP_K,))],
        compiler_params=_params("arbitrary"),
        name="combine_ln",
    )(pos.reshape(t // tc, 1, tc * TOP_K), ys, res, ln_w.reshape(1, d), ln_b.reshape(1, d))


def _moe_ln(xf, xb, w_router, w1, w3, w2, ln_w, ln_b, alpha, tm_target=1024):
    t = xf.shape[0]
    n_exp = w_router.shape[1]
    tm = _tile(TOP_K * t, tm_target, 8)
    top_w, top_e = _router_top2(xf, w_router)
    src_tok, row_gate, tile_expert, n_used, pos = _route(top_w, top_e, n_exp, tm)
    xs = _gather_rows(_pack_halves(xb), src_tok, tm)
    h = _grouped_up(xs, w1, w3, tile_expert, n_used, tm)
    ys = _grouped_down(h, w2, row_gate, tile_expert, n_used, tm)
    return _combine_ln(ys, pos, xf, ln_w, ln_b, alpha)


def _rwkv_kernel(r_ref, w_ref, k_ref, v_ref, a_ref, b_ref, y_ref, s_ref, *, tt, n, sub):
    @pl.when(pl.program_id(0) == 0)
    def _():
        s_ref[...] = jnp.zeros_like(s_ref)

    nblk = n // sub
    zero = jnp.zeros(v_ref.shape[1:], _F32)

    def step(t, carry):
        def sa_body(j, acc):
            a_blk = a_ref[t, pl.ds(pl.multiple_of(j * sub, sub), sub), :]
            for u in range(sub):
                acc = acc + s_ref[j * sub + u] * a_blk[u:u + 1, :]
            return acc

        sa = lax.fori_loop(0, nblk, sa_body, zero)
        v_t = v_ref[t]

        def upd_body(j, y):
            rows = pl.ds(pl.multiple_of(j * sub, sub), sub)
            w_blk = w_ref[t, rows, :]
            b_blk = b_ref[t, rows, :]
            k_blk = k_ref[t, rows, :]
            r_blk = r_ref[t, rows, :]
            for u in range(sub):
                s_new = s_ref[j * sub + u] * w_blk[u:u + 1, :] + sa * b_blk[u:u + 1, :] + v_t * k_blk[u:u + 1, :]
                s_ref[j * sub + u] = s_new
                y = y + s_new * r_blk[u:u + 1, :]
            return y

        y_ref[t] = lax.fori_loop(0, nblk, upd_body, zero)
        return carry

    lax.fori_loop(0, tt, step, 0)


def _rwkv_scan(r, w, k, v, a, b, tt_target=32):
    s, n, l = r.shape
    tt = _tile(s, tt_target, 1)
    sub = 8 if n % 8 == 0 else 1
    spec = pl.BlockSpec((tt, n, l), lambda i: (i, 0, 0))
    return pl.pallas_call(
        functools.partial(_rwkv_kernel, tt=tt, n=n, sub=sub),
        grid=(s // tt,),
        in_specs=[spec] * 6,
        out_specs=spec,
        out_shape=jax.ShapeDtypeStruct((s, n, l), _F32),
        scratch_shapes=[pltpu.VMEM((n, n, l), _F32)],
        compiler_params=_params("arbitrary"),
        name="rwkv7_scan",
    )(r, w, k, v, a, b)


def _hgrn_kernel(q_ref, f_ref, k_ref, i_ref, o_ref, s_ref, *, tt, dk, sub):
    @pl.when(pl.program_id(0) == 0)
    def _():
        s_ref[...] = jnp.zeros_like(s_ref)

    nblk = dk // sub
    zero = jnp.zeros(i_ref.shape[1:], _F32)

    def step(t, carry):
        i_t = i_ref[t]

        def body(j, o):
            rows = pl.ds(pl.multiple_of(j * sub, sub), sub)
            f_blk = f_ref[t, rows, :]
            k_blk = k_ref[t, rows, :]
            q_blk = q_ref[t, rows, :]
            for u in range(sub):
                s_new = s_ref[j * sub + u] * f_blk[u:u + 1, :] + i_t * k_blk[u:u + 1, :]
                s_ref[j * sub + u] = s_new
                o = o + s_new * q_blk[u:u + 1, :]
            return o

        o_ref[t] = lax.fori_loop(0, nblk, body, zero)
        return carry

    lax.fori_loop(0, tt, step, 0)


def _hgrn_scan(q, f, k, i, tt_target=32):
    s, dk, l = q.shape
    dv = i.shape[1]
    tt = _tile(s, tt_target, 1)
    sub = 8 if dk % 8 == 0 else 1
    kspec = pl.BlockSpec((tt, dk, l), lambda t: (t, 0, 0))
    vspec = pl.BlockSpec((tt, dv, l), lambda t: (t, 0, 0))
    return pl.pallas_call(
        functools.partial(_hgrn_kernel, tt=tt, dk=dk, sub=sub),
        grid=(s // tt,),
        in_specs=[kspec, kspec, kspec, vspec],
        out_specs=vspec,
        out_shape=jax.ShapeDtypeStruct((s, dv, l), _F32),
        scratch_shapes=[pltpu.VMEM((dk, dv, l), _F32)],
        compiler_params=_params("arbitrary"),
        name="hgrn2_scan",
    )(q, f, k, i)


def _group_matrix(width, group):
    col = jnp.arange(width, dtype=jnp.int32)[:, None] // group
    return (col == jnp.arange(width // group, dtype=jnp.int32)[None, :]).astype(_F32)


def _gsum(x, g):
    return jnp.dot(x, g, preferred_element_type=_F32, precision=lax.Precision.HIGHEST)


def _prep_kernel(*refs, first_layer, ranks):
    if first_layer:
        (hf_ref, k_ref, t_ref, lb_ref, vec_ref, wt_ref, g_ref, gt_ref,
         f_out, kin_out, w_out, k_out, a_out, b_out, gate_out) = refs
    else:
        (hf_ref, k_ref, t_ref, v_ref, vf_ref, lb_ref, vec_ref, wt_ref, g_ref, gt_ref,
         f_out, kin_out, w_out, k_out, a_out, b_out, gate_out, v_out) = refs
    width = k_ref.shape[1]
    d_rank, a_rank, g_rank = ranks

    hf = hf_ref[...]
    lb = lb_ref[...]
    f_out[...] = lb + (1.0 - lb) * jax.nn.sigmoid(hf)
    kin_out[...] = (1.0 - lb) * jax.nn.sigmoid(-hf)

    t = t_ref[...]
    col = lax.broadcasted_iota(jnp.int32, t.shape, 1)
    act = jnp.where(col < d_rank, jnp.tanh(t),
                    jnp.where(jnp.logical_and(col >= d_rank + a_rank, col < d_rank + a_rank + g_rank),
                              jax.nn.sigmoid(t), t))
    low = jnp.dot(act.astype(_BF16), wt_ref[...], preferred_element_type=_F32)
    w0, a0, v0, k_k, k_a = (vec_ref[n:n + 1, :] for n in range(5))
    w_log = -jax.nn.softplus(-(w0 + low[:, :width])) - 0.5
    w_out[...] = jnp.exp(-jnp.exp(w_log))
    iclr = jax.nn.sigmoid(a0 + low[:, width:2 * width])
    gate_out[...] = low[:, 2 * width:3 * width]
    if not first_layer:
        v = v_ref[...]
        v_out[...] = v + (vf_ref[...] - v) * jax.nn.sigmoid(v0 + low[:, 3 * width:])

    k = k_ref[...]
    kk = k * k_k
    inv = 1.0 / jnp.maximum(jnp.sqrt(_gsum(kk * kk, g_ref[...])), L2_EPS)
    kk = kk * _gsum(inv, gt_ref[...])
    k_out[...] = k * (1.0 + (iclr - 1.0) * k_a)
    a_out[...] = -kk
    b_out[...] = kk * iclr


def _mixer_prep(p, tail, p_first, lb, vecs, w_tail, ranks, hw, rw, r_n, first_layer, tm_target=256):
    m = p.shape[0]
    tm = _tile(m, tm_target, 8)
    hb = 4 * hw // rw
    blk = lambda c: pl.BlockSpec((tm, rw), lambda i: (i, c))
    full = lambda a: pl.BlockSpec(a.shape, lambda i: (0,) * a.ndim)
    g = _group_matrix(rw, r_n)
    gt = g.T
    n_out = 7 if first_layer else 8
    in_specs = [pl.BlockSpec((tm, hw), lambda i: (i, 1)), blk(hb + 1), pl.BlockSpec((tm, tail.shape[1]), lambda i: (i, 0))]
    args = [p, p, tail]
    if not first_layer:
        in_specs += [blk(hb + 2), blk(hb + 2)]
        args += [p, p_first]
    consts = [lb.reshape(1, hw), vecs, w_tail, g, gt]
    in_specs += [full(a) for a in consts]
    return pl.pallas_call(
        functools.partial(_prep_kernel, first_layer=first_layer, ranks=ranks),
        grid=(m // tm,),
        in_specs=in_specs,
        out_specs=[pl.BlockSpec((tm, rw), lambda i: (i, 0))] * n_out,
        out_shape=[jax.ShapeDtypeStruct((m, rw), _F32)] * n_out,
        compiler_params=_params("arbitrary"),
        name="mixer_prep",
    )(*args, *consts)


def _post_kernel(oh_ref, hg_ref, y_ref, r_ref, k_ref, v_ref, gate_ref, vec_ref, nw_ref,
                 gh_ref, ght_ref, gr_ref, grt_ref, o_ref, *, h_dim, r_dim):
    oh = oh_ref[...]
    hg = hg_ref[...]
    rs = lax.rsqrt(_gsum(oh * oh, gh_ref[...]) * (1.0 / h_dim) + RMS_EPS)
    o_h = oh * _gsum(rs, ght_ref[...]) * nw_ref[...] * hg * jax.nn.sigmoid(hg)

    lnx_w, lnx_b, r_k = (vec_ref[n:n + 1, :] for n in range(3))
    y = y_ref[...]
    gr = gr_ref[...]
    grt = grt_ref[...]
    yc = y - _gsum(_gsum(y, gr) * (1.0 / r_dim), grt)
    rs_y = lax.rsqrt(_gsum(yc * yc, gr) * (1.0 / r_dim) + GN_EPS)
    bonus = _gsum(_gsum(r_ref[...] * k_ref[...] * r_k, gr), grt) * v_ref[...]
    o_r = (yc * _gsum(rs_y, grt) * lnx_w + lnx_b + bonus) * gate_ref[...]
    o_ref[...] = jnp.concatenate([o_h, o_r], axis=1).astype(o_ref.dtype)


def _mixer_post(oh, y, p, k, v_src, v_blk, gate, vecs, norm_w, hw, rw, h_dim, r_dim, tm_target=256):
    m = oh.shape[0]
    tm = _tile(m, tm_target, 8)
    hb = 4 * hw // rw
    row = lambda w, c=0: pl.BlockSpec((tm, w), lambda i: (i, c))
    full = lambda a: pl.BlockSpec(a.shape, lambda i: (0,) * a.ndim)
    gh = _group_matrix(hw, h_dim)
    gr = _group_matrix(rw, r_dim)
    consts = [vecs, jnp.tile(norm_w, hw // h_dim).reshape(1, hw), gh, gh.T, gr, gr.T]
    return pl.pallas_call(
        functools.partial(_post_kernel, h_dim=h_dim, r_dim=r_dim),
        grid=(m // tm,),
        in_specs=[row(hw), row(hw, 3), row(rw), row(rw, hb), row(rw), row(rw, v_blk), row(rw)] + [full(a) for a in consts],
        out_specs=pl.BlockSpec((tm, hw + rw), lambda i: (i, 0)),
        out_shape=jax.ShapeDtypeStruct((m, hw + rw), _BF16),
        compiler_params=_params("arbitrary"),
        name="mixer_post",
    )(oh, p, y, p, k, v_src, gate, *consts)


def _to_scan(t, bsz, heads):
    m, width = t.shape
    seq, n = m // bsz, width // heads
    return t.reshape(bsz, seq, heads, n).transpose(1, 3, 0, 2).reshape(seq, n, bsz * heads)


def _from_scan(t, bsz, heads):
    seq, n, _ = t.shape
    return t.reshape(seq, n, bsz, heads).transpose(2, 0, 3, 1).reshape(bsz * seq, heads * n)


def _hgrn_recurrence(q, f, k_in, i, bsz, heads):
    m, width = q.shape
    seq, dk = m // bsz, width // heads
    halves = 2 if (dk % 2 == 0 and bsz * heads * 2 <= LANES) else 1
    dv_lo = dk // halves

    def key_layout(t):
        t = _to_scan(t, bsz, heads)
        return jnp.concatenate([t] * halves, axis=-1)

    i_s = i.reshape(bsz, seq, heads, halves, dv_lo).transpose(1, 4, 3, 0, 2).reshape(seq, dv_lo, halves * bsz * heads)
    o = _hgrn_scan(key_layout(q), key_layout(f), key_layout(k_in), i_s)
    return o.reshape(seq, dv_lo, halves, bsz, heads).transpose(3, 0, 4, 2, 1).reshape(m, width)


def kernel(x, w_in, w_in_vres, hgrn_lower_bounds, hgrn_norm_w, rwkv_mu, rwkv_mu_vres, rwkv_w0, rwkv_w2, rwkv_a0, rwkv_a2, rwkv_v0, rwkv_v2, rwkv_g2, rwkv_k_k, rwkv_k_a, rwkv_r_k, rwkv_lnx_w, rwkv_lnx_b, w_out, ln1_w, ln1_b, ffn_w1, ffn_w3, ffn_w2, moe_router, moe_w1, moe_w3, moe_w2, ln2_w, ln2_b):
    bsz, seq, d = x.shape
    depth = w_in.shape[0]
    alpha = (2 * depth) ** 0.25
    hw = hgrn_lower_bounds.shape[1]
    h_dim = hgrn_norm_w.shape[1]
    rw = rwkv_w0.shape[1]
    r_heads, r_dim = rwkv_r_k.shape[1], rwkv_r_k.shape[2]
    assert hw == rw and hw % LANES == 0
    ranks = (rwkv_w2.shape[1], rwkv_a2.shape[1], rwkv_g2.shape[1])
    v_rank = rwkv_v2.shape[1]
    main = 4 * hw + 3 * rw
    tail_w = -(-(sum(ranks) + v_rank) // LANES) * LANES
    hb = 4 * hw // rw
    m = bsz * seq

    lb_all = jax.nn.softmax(hgrn_lower_bounds.astype(_F32), axis=0)
    lb_all = jnp.cumsum(lb_all, axis=0) - lb_all[0:1]

    xf = x.reshape(m, d)
    xb = xf.astype(_BF16)
    p_first = None
    for layer in range(depth):
        first = layer == 0
        w_tail_in, mu = w_in[layer][:, main:], rwkv_mu[layer]
        if not first:
            w_tail_in = jnp.concatenate([w_tail_in, w_in_vres[layer - 1]], axis=1)
            mu = jnp.concatenate([mu, rwkv_mu_vres[layer - 1]], axis=0)
        pad = tail_w - w_tail_in.shape[1]
        w_tail_in = jnp.pad(w_tail_in, ((0, 0), (0, pad)))
        mu = jnp.concatenate([jnp.zeros((4 * hw,), _F32), mu.astype(_F32), jnp.zeros((pad,), _F32)])
        p = _in_proj(xb, w_in[layer][:, :main].astype(_BF16), mu[:main], seq, 4 * hw)
        tail = _in_proj(xb, w_tail_in.astype(_BF16), mu[main:], seq, 0)

        blocks = [rwkv_w2[layer], rwkv_a2[layer], rwkv_g2[layer]]
        if not first:
            blocks.append(rwkv_v2[layer - 1])
        w_tail_out = jnp.zeros((tail_w, 4 * rw), _F32)
        r0 = 0
        for n, blk in enumerate(blocks):
            w_tail_out = w_tail_out.at[r0:r0 + blk.shape[0], n * rw:(n + 1) * rw].set(blk)
            r0 += blk.shape[0]
        v0 = jnp.zeros((rw,), _F32) if first else rwkv_v0[layer - 1]
        vecs = jnp.stack([rwkv_w0[layer], rwkv_a0[layer], v0, rwkv_k_k[layer], rwkv_k_a[layer]]).astype(_F32)
        vecs = jnp.pad(vecs, ((0, 3), (0, 0)))
        outs = _mixer_prep(p, tail, p_first, lb_all[layer], vecs, w_tail_out.astype(_BF16), ranks,
                           hw, rw, r_dim, first)
        if first:
            f, k_in, decay, k, a, b, gate = outs
            p_first = p
            v_src, v_blk = p, hb + 2
            v = p[:, (hb + 2) * rw:(hb + 3) * rw]
        else:
            f, k_in, decay, k, a, b, gate, v = outs
            v_src, v_blk = v, 0

        oh = _hgrn_recurrence(p[:, :hw], f, k_in, p[:, 2 * hw:3 * hw], bsz, hw // h_dim)
        r = p[:, hb * rw:(hb + 1) * rw]
        y = _rwkv_scan(*[_to_scan(t, bsz, r_heads) for t in (r, decay, k, v, a, b)])
        y = _from_scan(y, bsz, r_heads)

        post_vecs = jnp.stack([rwkv_lnx_w[layer], rwkv_lnx_b[layer], rwkv_r_k[layer].reshape(rw)]).astype(_F32)
        post_vecs = jnp.pad(post_vecs, ((0, 5), (0, 0)))
        mix = _mixer_post(oh, y, p, k, v_src, v_blk, gate, post_vecs, hgrn_norm_w[layer].astype(_F32),
                          hw, rw, h_dim, r_dim)
        xf, xb = _down_ln(mix, w_out[layer].astype(_BF16), xf, ln1_w[layer], ln1_b[layer], alpha)

        j = layer // 2
        if layer % 2 == 0:
            h = _swiglu_up(xb, ffn_w1[j].astype(_BF16), ffn_w3[j].astype(_BF16))
            xf, xb = _down_ln(h, ffn_w2[j].astype(_BF16), xf, ln2_w[layer], ln2_b[layer], alpha)
        else:
            xf, xb = _moe_ln(xf, xb, moe_router[j], moe_w1[j], moe_w3[j], moe_w2[j], ln2_w[layer], ln2_b[layer], alpha)
    return xf.reshape(bsz, seq, d)
```

```python
import functools

import jax
import jax.numpy as jnp
from jax import lax
from jax.experimental import pallas as pl
from jax.experimental.pallas import tpu as pltpu

LN_EPS = 1e-5
RMS_EPS = 1e-5
GN_EPS = 64e-5
L2_EPS = 1e-12
TOP_K = 2

LANES = 128
VMEM_LIMIT_BYTES = 56 * 1024 * 1024

_F32 = jnp.float32
_BF16 = jnp.bfloat16


def _tile(n, target, align):
    best = None
    for t in range(align, min(n, target) + 1, align):
        if n % t == 0:
            best = t
    return best if best is not None else n


def _params(*sem):
    return pltpu.CompilerParams(dimension_semantics=sem, vmem_limit_bytes=VMEM_LIMIT_BYTES)


def _in_proj_kernel(x_ref, w_ref, mu_ref, o_ref, last_ref, *, shift_from, tiles_per_seq):
    j = pl.program_id(0)
    i = pl.program_id(1)
    acc = jnp.dot(x_ref[...], w_ref[...], preferred_element_type=_F32)

    @pl.when(j < shift_from)
    def _():
        o_ref[...] = acc

    @pl.when(j >= shift_from)
    def _():
        carry = jnp.where(i % tiles_per_seq == 0, 0.0, last_ref[...])
        row = lax.broadcasted_iota(jnp.int32, acc.shape, 0)
        prev = jnp.where(row == 0, carry, pltpu.roll(acc, 1, axis=0))
        last_ref[...] = acc[acc.shape[0] - 1:, :]
        o_ref[...] = acc + mu_ref[...] * (prev - acc)


def _in_proj(x, w, mu, seq, shift_from_col, tm_target=1024, tn_target=512):
    m, k = x.shape
    n = w.shape[1]
    tm = _tile(seq, tm_target, 8)
    tn = _tile(n, tn_target, LANES)
    while shift_from_col % tn:
        tn -= LANES
    return pl.pallas_call(
        functools.partial(_in_proj_kernel, shift_from=shift_from_col // tn, tiles_per_seq=seq // tm),
        grid=(n // tn, m // tm),
        in_specs=[
            pl.BlockSpec((tm, k), lambda j, i: (i, 0)),
            pl.BlockSpec((k, tn), lambda j, i: (0, j)),
            pl.BlockSpec((1, tn), lambda j, i: (0, j)),
        ],
        out_specs=pl.BlockSpec((tm, tn), lambda j, i: (i, j)),
        out_shape=jax.ShapeDtypeStruct((m, n), _F32),
        scratch_shapes=[pltpu.VMEM((1, tn), _F32)],
        compiler_params=_params("arbitrary", "arbitrary"),
        name="in_proj",
    )(x, w, mu.reshape(1, n).astype(_F32))


def _up_kernel(x_ref, w1_ref, w3_ref, o_ref):
    x = x_ref[...]
    a = jnp.dot(x, w1_ref[...], preferred_element_type=_F32)
    b = jnp.dot(x, w3_ref[...], preferred_element_type=_F32)
    o_ref[...] = (a * jax.nn.sigmoid(a) * b).astype(o_ref.dtype)


def _swiglu_up(x, w1, w3, tm_target=1024, tn_target=512):
    m, k = x.shape
    f = w1.shape[1]
    tm = _tile(m, tm_target, 8)
    tn = _tile(f, tn_target, LANES)
    w_spec = pl.BlockSpec((k, tn), lambda j, i: (0, j))
    return pl.pallas_call(
        _up_kernel,
        grid=(f // tn, m // tm),
        in_specs=[pl.BlockSpec((tm, k), lambda j, i: (i, 0)), w_spec, w_spec],
        out_specs=pl.BlockSpec((tm, tn), lambda j, i: (i, j)),
        out_shape=jax.ShapeDtypeStruct((m, f), _BF16),
        compiler_params=_params("arbitrary", "arbitrary"),
        name="swiglu_up",
    )(x, w1, w3)


def _down_ln_kernel(h_ref, w_ref, res_ref, lw_ref, lb_ref, o_ref, ob_ref, *, alpha, nk):
    kk = pl.program_id(1)
    part = jnp.dot(h_ref[...], w_ref[...], preferred_element_type=_F32)

    @pl.when(kk == 0)
    def _():
        o_ref[...] = part

    @pl.when(kk > 0)
    def _():
        o_ref[...] += part

    @pl.when(kk == nk - 1)
    def _():
        y = alpha * res_ref[...] + o_ref[...]
        mu = jnp.mean(y, axis=-1, keepdims=True)
        yc = y - mu
        var = jnp.mean(yc * yc, axis=-1, keepdims=True)
        out = yc * lax.rsqrt(var + LN_EPS) * lw_ref[...] + lb_ref[...]
        o_ref[...] = out
        ob_ref[...] = out.astype(_BF16)


def _down_ln(h, w, res, ln_w, ln_b, alpha, tm_target=512, tk_target=512):
    m = h.shape[0]
    f, d = w.shape
    tm = _tile(m, tm_target, 8)
    tk = _tile(f, tk_target, LANES)
    nk = f // tk
    return pl.pallas_call(
        functools.partial(_down_ln_kernel, alpha=alpha, nk=nk),
        grid=(m // tm, nk),
        in_specs=[
            pl.BlockSpec((tm, tk), lambda i, kk: (i, kk)),
            pl.BlockSpec((tk, d), lambda i, kk: (kk, 0)),
            pl.BlockSpec((tm, d), lambda i, kk: (i, 0)),
            pl.BlockSpec((1, d), lambda i, kk: (0, 0)),
            pl.BlockSpec((1, d), lambda i, kk: (0, 0)),
        ],
        out_specs=[
            pl.BlockSpec((tm, d), lambda i, kk: (i, 0)),
            pl.BlockSpec((tm, d), lambda i, kk: (i, 0)),
        ],
        out_shape=[jax.ShapeDtypeStruct((m, d), _F32), jax.ShapeDtypeStruct((m, d), _BF16)],
        compiler_params=_params("arbitrary", "arbitrary"),
        name="down_ln",
    )(h, w, res, ln_w.reshape(1, d), ln_b.reshape(1, d))


def _router_kernel(x_ref, w_ref, g_ref, e_ref):
    logits = jnp.dot(x_ref[...], w_ref[...], preferred_element_type=_F32, precision=lax.Precision.HIGHEST)
    n_exp = logits.shape[1]
    col = lax.broadcasted_iota(jnp.int32, logits.shape, 1)
    m1 = jnp.max(logits, axis=1, keepdims=True)
    i1 = jnp.min(jnp.where(logits == m1, col, n_exp), axis=1, keepdims=True)
    rest = jnp.where(col == i1, -jnp.inf, logits)
    m2 = jnp.max(rest, axis=1, keepdims=True)
    i2 = jnp.min(jnp.where(rest == m2, col, n_exp), axis=1, keepdims=True)
    ex = jnp.exp(m2 - m1)
    den = 1.0 + ex
    two = lax.broadcasted_iota(jnp.int32, g_ref.shape, 1)
    g_ref[...] = jnp.where(two == 0, 1.0 / den, ex / den)
    e_ref[...] = jnp.where(two == 0, i1, i2)


def _router_top2(x, w_router, tm_target=512):
    m, d = x.shape
    e = w_router.shape[1]
    tm = _tile(m, tm_target, 8)
    return pl.pallas_call(
        _router_kernel,
        grid=(m // tm,),
        in_specs=[pl.BlockSpec((tm, d), lambda i: (i, 0)), pl.BlockSpec((d, e), lambda i: (0, 0))],
        out_specs=[pl.BlockSpec((tm, TOP_K), lambda i: (i, 0)), pl.BlockSpec((tm, TOP_K), lambda i: (i, 0))],
        out_shape=[jax.ShapeDtypeStruct((m, TOP_K), _F32), jax.ShapeDtypeStruct((m, TOP_K), jnp.int32)],
        compiler_params=_params("arbitrary"),
        name="router_top2",
    )(x, w_router)


def _route(top_w, top_e, n_exp, tm):
    t = top_e.shape[0]
    slots = TOP_K * t
    nt = slots // tm + n_exp
    flat_e = top_e.reshape(slots)
    onehot = (flat_e[:, None] == jnp.arange(n_exp, dtype=jnp.int32)[None, :]).astype(jnp.int32)
    counts = jnp.sum(onehot, axis=0)
    rank = jnp.sum((jnp.cumsum(onehot, axis=0) - onehot) * onehot, axis=1)
    padded = ((counts + tm - 1) // tm) * tm
    ends = jnp.cumsum(padded)
    starts = ends - padded
    pos = starts[flat_e] + rank
    order = jnp.argsort(pos).astype(jnp.int32)
    first_slot = jnp.cumsum(counts) - counts
    tile_expert = jnp.minimum(
        jnp.searchsorted(ends, jnp.arange(nt, dtype=jnp.int32) * tm, side="right"), n_exp - 1).astype(jnp.int32)
    row = jnp.arange(nt * tm, dtype=jnp.int32)
    row_e = tile_expert[row // tm]
    row_rank = row - starts[row_e]
    valid = row_rank < counts[row_e]
    slot = order[jnp.clip(first_slot[row_e] + row_rank, 0, slots - 1)]
    src_tok = jnp.where(valid, slot // TOP_K, 0).astype(jnp.int32)
    row_gate = jnp.where(valid, top_w.reshape(slots)[slot], 0.0)
    n_used = (ends[-1] // tm).astype(jnp.int32).reshape(1)
    return src_tok, row_gate, tile_expert, n_used, pos.astype(jnp.int32).reshape(t, TOP_K)


def _gather_kernel(idx_ref, src_ref, o_ref, sem, *, tm):
    def issue(r, carry):
        pltpu.make_async_copy(src_ref.at[pl.ds(idx_ref[0, r], 1)], o_ref.at[pl.ds(r, 1)], sem).start()
        return carry

    lax.fori_loop(0, tm, issue, 0, unroll=8)
    pltpu.make_async_copy(src_ref.at[pl.ds(0, tm)], o_ref, sem).wait()


def _gather_rows(src, idx, tm):
    p = idx.shape[0]
    w = src.shape[1]
    return pl.pallas_call(
        functools.partial(_gather_kernel, tm=tm),
        grid=(p // tm,),
        in_specs=[
            pl.BlockSpec((None, 1, tm), lambda i: (i, 0, 0), memory_space=pltpu.SMEM),
            pl.BlockSpec(memory_space=pl.ANY),
        ],
        out_specs=pl.BlockSpec((tm, w), lambda i: (i, 0)),
        out_shape=jax.ShapeDtypeStruct((p, w), src.dtype),
        scratch_shapes=[pltpu.SemaphoreType.DMA(())],
        compiler_params=_params("arbitrary"),
        name="gather_rows",
    )(idx.reshape(p // tm, 1, tm), src)


def _pack_halves(xb):
    half = xb.shape[1] // 2
    lo = lax.bitcast_convert_type(xb[:, :half], jnp.uint16).astype(jnp.uint32)
    hi = lax.bitcast_convert_type(xb[:, half:], jnp.uint16).astype(jnp.uint32)
    return lo | (hi << 16)


def _unpack_halves(u):
    lo = pltpu.bitcast(u << 16, _F32).astype(_BF16)
    hi = pltpu.bitcast(u & jnp.uint32(0xFFFF0000), _F32).astype(_BF16)
    return jnp.concatenate([lo, hi], axis=1)


def _gup_kernel(te_ref, nu_ref, x_ref, w1_ref, w3_ref, o_ref, w1b_ref, w3b_ref):
    i = pl.program_id(1)
    new_expert = jnp.logical_or(i == 0, te_ref[i] != te_ref[jnp.maximum(i - 1, 0)])

    @pl.when(new_expert)
    def _():
        w1b_ref[...] = w1_ref[...].astype(_BF16)
        w3b_ref[...] = w3_ref[...].astype(_BF16)

    @pl.when(i < nu_ref[0])
    def _():
        x = _unpack_halves(x_ref[...])
        a = jnp.dot(x, w1b_ref[...], preferred_element_type=_F32)
        b = jnp.dot(x, w3b_ref[...], preferred_element_type=_F32)
        o_ref[...] = (a * jax.nn.sigmoid(a) * b).astype(o_ref.dtype)

    @pl.when(i >= nu_ref[0])
    def _():
        o_ref[...] = jnp.zeros_like(o_ref)


def _grouped_up(xs, w1, w3, tile_expert, n_used, tm, tn_target=512):
    p = xs.shape[0]
    _, d, f = w1.shape
    tn = _tile(f, tn_target, LANES)
    w_spec = pl.BlockSpec((None, d, tn), lambda j, i, te, nu: (te[i], 0, j))
    return pl.pallas_call(
        _gup_kernel,
        grid_spec=pltpu.PrefetchScalarGridSpec(
            num_scalar_prefetch=2,
            grid=(f // tn, p // tm),
            in_specs=[pl.BlockSpec((tm, xs.shape[1]), lambda j, i, te, nu: (i, 0)), w_spec, w_spec],
            out_specs=pl.BlockSpec((tm, tn), lambda j, i, te, nu: (i, j)),
            scratch_shapes=[pltpu.VMEM((d, tn), _BF16), pltpu.VMEM((d, tn), _BF16)],
        ),
        out_shape=jax.ShapeDtypeStruct((p, f), _BF16),
        compiler_params=_params("arbitrary", "arbitrary"),
        name="moe_up",
    )(tile_expert, n_used, xs, w1, w3)


def _gdown_kernel(te_ref, nu_ref, h_ref, w_ref, g_ref, o_ref, *, nk):
    i = pl.program_id(0)
    kk = pl.program_id(1)
    used = i < nu_ref[0]

    @pl.when(used)
    def _():
        part = jnp.dot(h_ref[...], w_ref[...].astype(_BF16), preferred_element_type=_F32)

        @pl.when(kk == 0)
        def _():
            o_ref[...] = part

        @pl.when(kk > 0)
        def _():
            o_ref[...] += part

        @pl.when(kk == nk - 1)
        def _():
            g = g_ref[...]
            o_ref[...] = o_ref[...] * jnp.concatenate([g] * (o_ref.shape[1] // g.shape[1]), axis=1)

    @pl.when(jnp.logical_and(jnp.logical_not(used), kk == nk - 1))
    def _():
        o_ref[...] = jnp.zeros_like(o_ref)


def _grouped_down(h, w2, row_gate, tile_expert, n_used, tm, tk_target=512):
    p, f = h.shape
    d = w2.shape[2]
    tk = _tile(f, tk_target, LANES)
    nk = f // tk
    gl = LANES if d % LANES == 0 else d
    gate = jnp.broadcast_to(row_gate[:, None], (p, gl))
    return pl.pallas_call(
        functools.partial(_gdown_kernel, nk=nk),
        grid_spec=pltpu.PrefetchScalarGridSpec(
            num_scalar_prefetch=2,
            grid=(p // tm, nk),
            in_specs=[
                pl.BlockSpec((tm, tk), lambda i, kk, te, nu: (i, kk)),
                pl.BlockSpec((None, tk, d), lambda i, kk, te, nu: (te[i], kk, 0)),
                pl.BlockSpec((tm, gl), lambda i, kk, te, nu: (i, 0)),
            ],
            out_specs=pl.BlockSpec((tm, d), lambda i, kk, te, nu: (i, 0)),
        ),
        out_shape=jax.ShapeDtypeStruct((p, d), _F32),
        compiler_params=_params("arbitrary", "arbitrary"),
        name="moe_down",
    )(tile_expert, n_used, h, w2, gate)


def _combine_ln_kernel(pos_ref, ys_ref, res_ref, lw_ref, lb_ref, o_ref, ob_ref, buf_ref, sem, *, alpha, tc):
    def issue(r, carry):
        for j in range(TOP_K):
            pltpu.make_async_copy(
                ys_ref.at[pl.ds(pos_ref[0, r * TOP_K + j], 1)], buf_ref.at[j, pl.ds(r, 1)], sem.at[j]).start()
        return carry

    lax.fori_loop(0, tc, issue, 0, unroll=4)
    y = alpha * res_ref[...]
    for j in range(TOP_K):
        pltpu.make_async_copy(ys_ref.at[pl.ds(0, tc)], buf_ref.at[j], sem.at[j]).wait()
        y = y + buf_ref[j]
    mu = jnp.mean(y, axis=-1, keepdims=True)
    yc = y - mu
    var = jnp.mean(yc * yc, axis=-1, keepdims=True)
    out = yc * lax.rsqrt(var + LN_EPS) * lw_ref[...] + lb_ref[...]
    o_ref[...] = out
    ob_ref[...] = out.astype(_BF16)


def _combine_ln(ys, pos, res, ln_w, ln_b, alpha, tc_target=256):
    t, d = res.shape
    tc = _tile(t, tc_target, 8)
    return pl.pallas_call(
        functools.partial(_combine_ln_kernel, alpha=alpha, tc=tc),
        grid=(t // tc,),
        in_specs=[
            pl.BlockSpec((None, 1, tc * TOP_K), lambda i: (i, 0, 0), memory_space=pltpu.SMEM),
            pl.BlockSpec(memory_space=pl.ANY),
            pl.BlockSpec((tc, d), lambda i: (i, 0)),
            pl.BlockSpec((1, d), lambda i: (0, 0)),
            pl.BlockSpec((1, d), lambda i: (0, 0)),
        ],
        out_specs=[pl.BlockSpec((tc, d), lambda i: (i, 0)), pl.BlockSpec((tc, d), lambda i: (i, 0))],
        out_shape=[jax.ShapeDtypeStruct((t, d), _F32), jax.ShapeDtypeStruct((t, d), _BF16)],
        scratch_shapes=[pltpu.VMEM((TOP_K, tc, d), _F32), pltpu.SemaphoreType.DMA((TOP_K,))],
        compiler_params=_params("arbitrary"),
        name="combine_ln",
    )(pos.reshape(t // tc, 1, tc * TOP_K), ys, res, ln_w.reshape(1, d), ln_b.reshape(1, d))


def _moe_ln(xf, xb, w_router, w1, w3, w2, ln_w, ln_b, alpha, tm_target=1024):
    t = xf.shape[0]
    n_exp = w_router.shape[1]
    tm = _tile(TOP_K * t, tm_target, 8)
    top_w, top_e = _router_top2(xf, w_router)
    src_tok, row_gate, tile_expert, n_used, pos = _route(top_w, top_e, n_exp, tm)
    xs = _gather_rows(_pack_halves(xb), src_tok, tm)
    h = _grouped_up(xs, w1, w3, tile_expert, n_used, tm)
    ys = _grouped_down(h, w2, row_gate, tile_expert, n_used, tm)
    return _combine_ln(ys, pos, xf, ln_w, ln_b, alpha)


def _rwkv_kernel(r_ref, w_ref, k_ref, v_ref, a_ref, b_ref, y_ref, s_ref, *, tt, n, sub):
    @pl.when(pl.program_id(0) == 0)
    def _():
        s_ref[...] = jnp.zeros_like(s_ref)

    nblk = n // sub
    zero = jnp.zeros(v_ref.shape[1:], _F32)

    def step(t, carry):
        def sa_body(j, acc):
            a_blk = a_ref[t, pl.ds(pl.multiple_of(j * sub, sub), sub), :]
            for u in range(sub):
                acc = acc + s_ref[j * sub + u] * a_blk[u:u + 1, :]
            return acc

        sa = lax.fori_loop(0, nblk, sa_body, zero)
        v_t = v_ref[t]

        def upd_body(j, y):
            rows = pl.ds(pl.multiple_of(j * sub, sub), sub)
            w_blk = w_ref[t, rows, :]
            b_blk = b_ref[t, rows, :]
            k_blk = k_ref[t, rows, :]
            r_blk = r_ref[t, rows, :]
            for u in range(sub):
                s_new = s_ref[j * sub + u] * w_blk[u:u + 1, :] + sa * b_blk[u:u + 1, :] + v_t * k_blk[u:u + 1, :]
                s_ref[j * sub + u] = s_new
                y = y + s_new * r_blk[u:u + 1, :]
            return y

        y_ref[t] = lax.fori_loop(0, nblk, upd_body, zero)
        return carry

    lax.fori_loop(0, tt, step, 0)


def _rwkv_scan(r, w, k, v, a, b, tt_target=32):
    s, n, l = r.shape
    tt = _tile(s, tt_target, 1)
    sub = 8 if n % 8 == 0 else 1
    spec = pl.BlockSpec((tt, n, l), lambda i: (i, 0, 0))
    return pl.pallas_call(
        functools.partial(_rwkv_kernel, tt=tt, n=n, sub=sub),
        grid=(s // tt,),
        in_specs=[spec] * 6,
        out_specs=spec,
        out_shape=jax.ShapeDtypeStruct((s, n, l), _F32),
        scratch_shapes=[pltpu.VMEM((n, n, l), _F32)],
        compiler_params=_params("arbitrary"),
        name="rwkv7_scan",
    )(r, w, k, v, a, b)


HGRN_CHUNK = 64
HGRN_SUB = 16


def _dot_nt(a, b):
    return lax.dot_general(a, b, (((1,), (1,)), ((), ())), preferred_element_type=_F32)


def _dot_tn(a, b):
    return lax.dot_general(a, b, (((0,), (0,)), ((), ())), preferred_element_type=_F32)


def _hgrn_chunk_kernel(q_ref, hf_ref, hi_ref, hg_ref, lb_ref, nw_ref, o_ref, st_ref, *, first_layer, chunk, sub):
    @pl.when(pl.program_id(2) == 0)
    def _():
        st_ref[...] = jnp.zeros_like(st_ref)

    ts, dk = q_ref.shape
    nsub = chunk // sub
    lb = lb_ref[...]
    tri = (lax.broadcasted_iota(jnp.int32, (chunk, chunk), 0)
           >= lax.broadcasted_iota(jnp.int32, (chunk, chunk), 1)).astype(_F32)
    local_row = lax.broadcasted_iota(jnp.int32, (chunk, 1), 0) % sub

    def rows_of_subchunks(t, s):
        return jnp.concatenate(
            [jnp.broadcast_to(t[n * sub + s:n * sub + s + 1, :], (sub, t.shape[1])) for n in range(nsub)], axis=0)

    for c in range(ts // chunk):
        rows = slice(c * chunk, (c + 1) * chunk)
        q, hf, hi = q_ref[rows, :], hf_ref[rows, :], hi_ref[rows, :]
        if first_layer:
            log_f = jax.nn.log_sigmoid(hf)
        else:
            log_f = jnp.log(lb + (1.0 - lb) * jax.nn.sigmoid(hf))
        kin = (1.0 - lb) * jax.nn.sigmoid(-hf)
        cum = jnp.dot(tri, log_f, preferred_element_type=_F32, precision=lax.Precision.HIGHEST)
        total = cum[chunk - 1:, :]
        hi_b = hi.astype(_BF16)

        st = st_ref[...]
        o = _dot_nt((q * jnp.exp(cum)).astype(_BF16), st.astype(_BF16))
        k_end = (kin * jnp.exp(total - cum)).astype(_BF16)
        st_ref[...] = st * jnp.exp(total) + _dot_tn(hi_b, k_end)

        parts = [jnp.zeros((sub, hi.shape[1]), _F32)]
        for n in range(1, nsub):
            start = cum[n * sub - 1:n * sub, :]
            q_n = q[n * sub:(n + 1) * sub, :] * jnp.exp(cum[n * sub:(n + 1) * sub, :] - start)
            k_n = kin[:n * sub, :] * jnp.exp(start - cum[:n * sub, :])
            a_n = _dot_nt(q_n.astype(_BF16), k_n.astype(_BF16))
            parts.append(jnp.dot(a_n.astype(_BF16), hi_b[:n * sub, :], preferred_element_type=_F32))
        o = o + jnp.concatenate(parts, axis=0)

        for s in range(sub):
            diff = cum - rows_of_subchunks(cum, s)
            decay = jnp.exp(jnp.where(local_row >= s, diff, -jnp.inf))
            score = jnp.sum(q * rows_of_subchunks(kin, s) * decay, axis=1, keepdims=True)
            o = o + score * rows_of_subchunks(hi, s)

        hg = hg_ref[rows, :]
        on = o * lax.rsqrt(jnp.mean(o * o, axis=-1, keepdims=True) + RMS_EPS) * nw_ref[...]
        o_ref[rows, :] = (on * hg * jax.nn.sigmoid(hg)).astype(o_ref.dtype)


def _hgrn_chunked(p, lb, norm_w, bsz, hw, h_dim, first_layer, ts_target=256):
    m = p.shape[0]
    seq = m // bsz
    heads = hw // h_dim
    chunk = min(HGRN_CHUNK, seq)
    ts = _tile(seq, ts_target, chunk)
    nst = seq // ts
    col = lambda sec: pl.BlockSpec((ts, h_dim), lambda b, h, s: (b * nst + s, sec * heads + h))
    return pl.pallas_call(
        functools.partial(_hgrn_chunk_kernel, first_layer=first_layer, chunk=chunk, sub=min(HGRN_SUB, chunk)),
        grid=(bsz, heads, nst),
        in_specs=[col(0), col(1), col(2), col(3),
                  pl.BlockSpec((1, h_dim), lambda b, h, s: (0, h)),
                  pl.BlockSpec((1, h_dim), lambda b, h, s: (0, 0))],
        out_specs=pl.BlockSpec((ts, h_dim), lambda b, h, s: (b * nst + s, h)),
        out_shape=jax.ShapeDtypeStruct((m, hw), _BF16),
        scratch_shapes=[pltpu.VMEM((h_dim, h_dim), _F32)],
        compiler_params=_params("arbitrary", "arbitrary", "arbitrary"),
        name="hgrn2_chunked",
    )(p, p, p, p, lb.reshape(1, hw), norm_w.reshape(1, h_dim))


def _group_matrix(width, group):
    col = jnp.arange(width, dtype=jnp.int32)[:, None] // group
    return (col == jnp.arange(width // group, dtype=jnp.int32)[None, :]).astype(_BF16)


def _gsum(x, g):
    hi = x.astype(_BF16)
    lo = (x - hi.astype(_F32)).astype(_BF16)
    return jnp.dot(hi, g, preferred_element_type=_F32) + jnp.dot(lo, g, preferred_element_type=_F32)


def _prep_kernel(*refs, first_layer, ranks):
    if first_layer:
        (k_ref, t_ref, vec_ref, wt_ref, g_ref, gt_ref,
         w_out, k_out, a_out, b_out, gate_out) = refs
    else:
        (k_ref, t_ref, v_ref, vf_ref, vec_ref, wt_ref, g_ref, gt_ref,
         w_out, k_out, a_out, b_out, gate_out, v_out) = refs
    width = k_ref.shape[1]
    d_rank, a_rank, g_rank = ranks

    t = t_ref[...]
    col = lax.broadcasted_iota(jnp.int32, t.shape, 1)
    act = jnp.where(col < d_rank, jnp.tanh(t),
                    jnp.where(jnp.logical_and(col >= d_rank + a_rank, col < d_rank + a_rank + g_rank),
                              jax.nn.sigmoid(t), t))
    low = jnp.dot(act.astype(_BF16), wt_ref[...], preferred_element_type=_F32)
    w0, a0, v0, k_k, k_a = (vec_ref[n:n + 1, :] for n in range(5))
    w_log = -jax.nn.softplus(-(w0 + low[:, :width])) - 0.5
    w_out[...] = jnp.exp(-jnp.exp(w_log))
    iclr = jax.nn.sigmoid(a0 + low[:, width:2 * width])
    gate_out[...] = low[:, 2 * width:3 * width]
    if not first_layer:
        v = v_ref[...]
        v_out[...] = v + (vf_ref[...] - v) * jax.nn.sigmoid(v0 + low[:, 3 * width:])

    k = k_ref[...]
    kk = k * k_k
    inv = 1.0 / jnp.maximum(jnp.sqrt(_gsum(kk * kk, g_ref[...])), L2_EPS)
    kk = kk * _gsum(inv, gt_ref[...])
    k_out[...] = k * (1.0 + (iclr - 1.0) * k_a)
    a_out[...] = -kk
    b_out[...] = kk * iclr


def _mixer_prep(p, tail, p_first, vecs, w_tail, ranks, hw, rw, r_n, first_layer, tm_target=256):
    m = p.shape[0]
    tm = _tile(m, tm_target, 8)
    hb = 4 * hw // rw
    blk = lambda c: pl.BlockSpec((tm, rw), lambda i: (i, c))
    full = lambda a: pl.BlockSpec(a.shape, lambda i: (0,) * a.ndim)
    g = _group_matrix(rw, r_n)
    gt = g.T
    n_out = 5 if first_layer else 6
    in_specs = [blk(hb + 1), pl.BlockSpec((tm, tail.shape[1]), lambda i: (i, 0))]
    args = [p, tail]
    if not first_layer:
        in_specs += [blk(hb + 2), blk(hb + 2)]
        args += [p, p_first]
    consts = [vecs, w_tail, g, gt]
    in_specs += [full(a) for a in consts]
    return pl.pallas_call(
        functools.partial(_prep_kernel, first_layer=first_layer, ranks=ranks),
        grid=(m // tm,),
        in_specs=in_specs,
        out_specs=[pl.BlockSpec((tm, rw), lambda i: (i, 0))] * n_out,
        out_shape=[jax.ShapeDtypeStruct((m, rw), _F32)] * n_out,
        compiler_params=_params("arbitrary"),
        name="mixer_prep",
    )(*args, *consts)


def _post_kernel(y_ref, r_ref, k_ref, v_ref, gate_ref, vec_ref, gr_ref, grt_ref, o_ref, *, r_dim):
    lnx_w, lnx_b, r_k = (vec_ref[n:n + 1, :] for n in range(3))
    y = y_ref[...]
    gr = gr_ref[...]
    grt = grt_ref[...]
    yc = y - _gsum(_gsum(y, gr) * (1.0 / r_dim), grt)
    rs_y = lax.rsqrt(_gsum(yc * yc, gr) * (1.0 / r_dim) + GN_EPS)
    bonus = _gsum(_gsum(r_ref[...] * k_ref[...] * r_k, gr), grt) * v_ref[...]
    o_ref[...] = ((yc * _gsum(rs_y, grt) * lnx_w + lnx_b + bonus) * gate_ref[...]).astype(o_ref.dtype)


def _mixer_post(y, p, k, v_src, v_blk, gate, vecs, hw, rw, r_dim, tm_target=512):
    m = y.shape[0]
    tm = _tile(m, tm_target, 8)
    hb = 4 * hw // rw
    row = lambda c=0: pl.BlockSpec((tm, rw), lambda i: (i, c))
    full = lambda a: pl.BlockSpec(a.shape, lambda i: (0,) * a.ndim)
    gr = _group_matrix(rw, r_dim)
    consts = [vecs, gr, gr.T]
    return pl.pallas_call(
        functools.partial(_post_kernel, r_dim=r_dim),
        grid=(m // tm,),
        in_specs=[row(), row(hb), row(), row(v_blk), row()] + [full(a) for a in consts],
        out_specs=row(),
        out_shape=jax.ShapeDtypeStruct((m, rw), _BF16),
        compiler_params=_params("arbitrary"),
        name="mixer_post",
    )(y, p, k, v_src, gate, *consts)


def _to_scan(t, bsz, heads):
    m, width = t.shape
    seq, n = m // bsz, width // heads
    return t.reshape(bsz, seq, heads, n).transpose(1, 3, 0, 2).reshape(seq, n, bsz * heads)


def _from_scan(t, bsz, heads):
    seq, n, _ = t.shape
    return t.reshape(seq, n, bsz, heads).transpose(2, 0, 3, 1).reshape(bsz * seq, heads * n)


def kernel(x, w_in, w_in_vres, hgrn_lower_bounds, hgrn_norm_w, rwkv_mu, rwkv_mu_vres, rwkv_w0, rwkv_w2, rwkv_a0, rwkv_a2, rwkv_v0, rwkv_v2, rwkv_g2, rwkv_k_k, rwkv_k_a, rwkv_r_k, rwkv_lnx_w, rwkv_lnx_b, w_out, ln1_w, ln1_b, ffn_w1, ffn_w3, ffn_w2, moe_router, moe_w1, moe_w3, moe_w2, ln2_w, ln2_b):
    bsz, seq, d = x.shape
    depth = w_in.shape[0]
    alpha = (2 * depth) ** 0.25
    hw = hgrn_lower_bounds.shape[1]
    h_dim = hgrn_norm_w.shape[1]
    rw = rwkv_w0.shape[1]
    r_heads, r_dim = rwkv_r_k.shape[1], rwkv_r_k.shape[2]
    assert hw == rw and hw % LANES == 0
    ranks = (rwkv_w2.shape[1], rwkv_a2.shape[1], rwkv_g2.shape[1])
    v_rank = rwkv_v2.shape[1]
    main = 4 * hw + 3 * rw
    tail_w = -(-(sum(ranks) + v_rank) // LANES) * LANES
    hb = 4 * hw // rw
    m = bsz * seq

    lb_all = jax.nn.softmax(hgrn_lower_bounds.astype(_F32), axis=0)
    lb_all = jnp.cumsum(lb_all, axis=0) - lb_all[0:1]

    xf = x.reshape(m, d)
    xb = xf.astype(_BF16)
    p_first = None
    for layer in range(depth):
        first = layer == 0
        w_tail_in, mu = w_in[layer][:, main:], rwkv_mu[layer]
        if not first:
            w_tail_in = jnp.concatenate([w_tail_in, w_in_vres[layer - 1]], axis=1)
            mu = jnp.concatenate([mu, rwkv_mu_vres[layer - 1]], axis=0)
        pad = tail_w - w_tail_in.shape[1]
        w_tail_in = jnp.pad(w_tail_in, ((0, 0), (0, pad)))
        mu = jnp.concatenate([jnp.zeros((4 * hw,), _F32), mu.astype(_F32), jnp.zeros((pad,), _F32)])
        p = _in_proj(xb, w_in[layer][:, :main].astype(_BF16), mu[:main], seq, 4 * hw)
        tail = _in_proj(xb, w_tail_in.astype(_BF16), mu[main:], seq, 0)

        blocks = [rwkv_w2[layer], rwkv_a2[layer], rwkv_g2[layer]]
        if not first:
            blocks.append(rwkv_v2[layer - 1])
        w_tail_out = jnp.zeros((tail_w, 4 * rw), _F32)
        r0 = 0
        for n, blk in enumerate(blocks):
            w_tail_out = w_tail_out.at[r0:r0 + blk.shape[0], n * rw:(n + 1) * rw].set(blk)
            r0 += blk.shape[0]
        v0 = jnp.zeros((rw,), _F32) if first else rwkv_v0[layer - 1]
        vecs = jnp.stack([rwkv_w0[layer], rwkv_a0[layer], v0, rwkv_k_k[layer], rwkv_k_a[layer]]).astype(_F32)
        vecs = jnp.pad(vecs, ((0, 3), (0, 0)))
        outs = _mixer_prep(p, tail, p_first, vecs, w_tail_out.astype(_BF16), ranks, hw, rw, r_dim, first)
        if first:
            decay, k, a, b, gate = outs
            p_first = p
            v_src, v_blk = p, hb + 2
            v = p[:, (hb + 2) * rw:(hb + 3) * rw]
        else:
            decay, k, a, b, gate, v = outs
            v_src, v_blk = v, 0

        o_h = _hgrn_chunked(p, lb_all[layer], hgrn_norm_w[layer].astype(_F32), bsz, hw, h_dim, first)
        r = p[:, hb * rw:(hb + 1) * rw]
        y = _rwkv_scan(*[_to_scan(t, bsz, r_heads) for t in (r, decay, k, v, a, b)])
        y = _from_scan(y, bsz, r_heads)

        post_vecs = jnp.stack([rwkv_lnx_w[layer], rwkv_lnx_b[layer], rwkv_r_k[layer].reshape(rw)]).astype(_F32)
        post_vecs = jnp.pad(post_vecs, ((0, 5), (0, 0)))
        o_r = _mixer_post(y, p, k, v_src, v_blk, gate, post_vecs, hw, rw, r_dim)
        mix = jnp.concatenate([o_h, o_r], axis=1)
        xf, xb = _down_ln(mix, w_out[layer].astype(_BF16), xf, ln1_w[layer], ln1_b[layer], alpha)

        j = layer // 2
        if layer % 2 == 0:
            h = _swiglu_up(xb, ffn_w1[j].astype(_BF16), ffn_w3[j].astype(_BF16))
            xf, xb = _down_ln(h, ffn_w2[j].astype(_BF16), xf, ln2_w[layer], ln2_b[layer], alpha)
        else:
            xf, xb = _moe_ln(xf, xb, moe_router[j], moe_w1[j], moe_w3[j], moe_w2[j], ln2_w[layer], ln2_b[layer], alpha)
    return xf.reshape(bsz, seq, d)
```

```python
import functools

import jax
import jax.numpy as jnp
from jax import lax
from jax.experimental import pallas as pl
from jax.experimental.pallas import tpu as pltpu

LN_EPS = 1e-5
RMS_EPS = 1e-5
GN_EPS = 64e-5
L2_EPS = 1e-12
TOP_K = 2

LANES = 128
VMEM_LIMIT_BYTES = 56 * 1024 * 1024

_F32 = jnp.float32
_BF16 = jnp.bfloat16


def _tile(n, target, align):
    best = None
    for t in range(align, min(n, target) + 1, align):
        if n % t == 0:
            best = t
    return best if best is not None else n


def _params(*sem):
    return pltpu.CompilerParams(dimension_semantics=sem, vmem_limit_bytes=VMEM_LIMIT_BYTES)


def _in_proj_kernel(x_ref, w_ref, mu_ref, o_ref, last_ref, *, shift_from, tiles_per_seq):
    j = pl.program_id(0)
    i = pl.program_id(1)
    acc = jnp.dot(x_ref[...], w_ref[...], preferred_element_type=_F32)

    @pl.when(j < shift_from)
    def _():
        o_ref[...] = acc

    @pl.when(j >= shift_from)
    def _():
        carry = jnp.where(i % tiles_per_seq == 0, 0.0, last_ref[...])
        row = lax.broadcasted_iota(jnp.int32, acc.shape, 0)
        prev = jnp.where(row == 0, carry, pltpu.roll(acc, 1, axis=0))
        last_ref[...] = acc[acc.shape[0] - 1:, :]
        o_ref[...] = acc + mu_ref[...] * (prev - acc)


def _in_proj(x, w, mu, seq, shift_from_col, tm_target=1024, tn_target=512):
    m, k = x.shape
    n = w.shape[1]
    tm = _tile(seq, tm_target, 8)
    tn = _tile(n, tn_target, LANES)
    while shift_from_col % tn:
        tn -= LANES
    return pl.pallas_call(
        functools.partial(_in_proj_kernel, shift_from=shift_from_col // tn, tiles_per_seq=seq // tm),
        grid=(n // tn, m // tm),
        in_specs=[
            pl.BlockSpec((tm, k), lambda j, i: (i, 0)),
            pl.BlockSpec((k, tn), lambda j, i: (0, j)),
            pl.BlockSpec((1, tn), lambda j, i: (0, j)),
        ],
        out_specs=pl.BlockSpec((tm, tn), lambda j, i: (i, j)),
        out_shape=jax.ShapeDtypeStruct((m, n), _F32),
        scratch_shapes=[pltpu.VMEM((1, tn), _F32)],
        compiler_params=_params("arbitrary", "arbitrary"),
        name="in_proj",
    )(x, w, mu.reshape(1, n).astype(_F32))


def _up_kernel(x_ref, w1_ref, w3_ref, o_ref):
    x = x_ref[...]
    a = jnp.dot(x, w1_ref[...], preferred_element_type=_F32)
    b = jnp.dot(x, w3_ref[...], preferred_element_type=_F32)
    o_ref[...] = (a * jax.nn.sigmoid(a) * b).astype(o_ref.dtype)


def _swiglu_up(x, w1, w3, tm_target=1024, tn_target=512):
    m, k = x.shape
    f = w1.shape[1]
    tm = _tile(m, tm_target, 8)
    tn = _tile(f, tn_target, LANES)
    w_spec = pl.BlockSpec((k, tn), lambda j, i: (0, j))
    return pl.pallas_call(
        _up_kernel,
        grid=(f // tn, m // tm),
        in_specs=[pl.BlockSpec((tm, k), lambda j, i: (i, 0)), w_spec, w_spec],
        out_specs=pl.BlockSpec((tm, tn), lambda j, i: (i, j)),
        out_shape=jax.ShapeDtypeStruct((m, f), _BF16),
        compiler_params=_params("arbitrary", "arbitrary"),
        name="swiglu_up",
    )(x, w1, w3)


def _down_ln_kernel(h_ref, w_ref, res_ref, lw_ref, lb_ref, o_ref, ob_ref, *, alpha, nk):
    kk = pl.program_id(1)
    part = jnp.dot(h_ref[...], w_ref[...], preferred_element_type=_F32)

    @pl.when(kk == 0)
    def _():
        o_ref[...] = part

    @pl.when(kk > 0)
    def _():
        o_ref[...] += part

    @pl.when(kk == nk - 1)
    def _():
        y = alpha * res_ref[...] + o_ref[...]
        mu = jnp.mean(y, axis=-1, keepdims=True)
        yc = y - mu
        var = jnp.mean(yc * yc, axis=-1, keepdims=True)
        out = yc * lax.rsqrt(var + LN_EPS) * lw_ref[...] + lb_ref[...]
        o_ref[...] = out
        ob_ref[...] = out.astype(_BF16)


def _down_ln(h, w, res, ln_w, ln_b, alpha, tm_target=512, tk_target=2048):
    m = h.shape[0]
    f, d = w.shape
    tm = _tile(m, tm_target, 8)
    tk = _tile(f, tk_target, LANES)
    nk = f // tk
    return pl.pallas_call(
        functools.partial(_down_ln_kernel, alpha=alpha, nk=nk),
        grid=(m // tm, nk),
        in_specs=[
            pl.BlockSpec((tm, tk), lambda i, kk: (i, kk)),
            pl.BlockSpec((tk, d), lambda i, kk: (kk, 0)),
            pl.BlockSpec((tm, d), lambda i, kk: (i, 0)),
            pl.BlockSpec((1, d), lambda i, kk: (0, 0)),
            pl.BlockSpec((1, d), lambda i, kk: (0, 0)),
        ],
        out_specs=[
            pl.BlockSpec((tm, d), lambda i, kk: (i, 0)),
            pl.BlockSpec((tm, d), lambda i, kk: (i, 0)),
        ],
        out_shape=[jax.ShapeDtypeStruct((m, d), _F32), jax.ShapeDtypeStruct((m, d), _BF16)],
        compiler_params=_params("arbitrary", "arbitrary"),
        name="down_ln",
    )(h, w, res, ln_w.reshape(1, d), ln_b.reshape(1, d))


def _router_kernel(x_ref, w_ref, g_ref, e_ref):
    logits = jnp.dot(x_ref[...], w_ref[...], preferred_element_type=_F32, precision=lax.Precision.HIGHEST)
    n_exp = logits.shape[1]
    col = lax.broadcasted_iota(jnp.int32, logits.shape, 1)
    m1 = jnp.max(logits, axis=1, keepdims=True)
    i1 = jnp.min(jnp.where(logits == m1, col, n_exp), axis=1, keepdims=True)
    rest = jnp.where(col == i1, -jnp.inf, logits)
    m2 = jnp.max(rest, axis=1, keepdims=True)
    i2 = jnp.min(jnp.where(rest == m2, col, n_exp), axis=1, keepdims=True)
    ex = jnp.exp(m2 - m1)
    den = 1.0 + ex
    two = lax.broadcasted_iota(jnp.int32, g_ref.shape, 1)
    g_ref[...] = jnp.where(two == 0, 1.0 / den, ex / den)
    e_ref[...] = jnp.where(two == 0, i1, i2)


def _router_top2(x, w_router, tm_target=512):
    m, d = x.shape
    e = w_router.shape[1]
    tm = _tile(m, tm_target, 8)
    return pl.pallas_call(
        _router_kernel,
        grid=(m // tm,),
        in_specs=[pl.BlockSpec((tm, d), lambda i: (i, 0)), pl.BlockSpec((d, e), lambda i: (0, 0))],
        out_specs=[pl.BlockSpec((tm, TOP_K), lambda i: (i, 0)), pl.BlockSpec((tm, TOP_K), lambda i: (i, 0))],
        out_shape=[jax.ShapeDtypeStruct((m, TOP_K), _F32), jax.ShapeDtypeStruct((m, TOP_K), jnp.int32)],
        compiler_params=_params("arbitrary"),
        name="router_top2",
    )(x, w_router)


def _route(top_w, top_e, n_exp, tm):
    t = top_e.shape[0]
    slots = TOP_K * t
    nt = slots // tm + n_exp
    flat_e = top_e.reshape(slots)
    onehot = (flat_e[:, None] == jnp.arange(n_exp, dtype=jnp.int32)[None, :]).astype(jnp.int32)
    counts = jnp.sum(onehot, axis=0)
    rank = jnp.sum((jnp.cumsum(onehot, axis=0) - onehot) * onehot, axis=1)
    padded = ((counts + tm - 1) // tm) * tm
    ends = jnp.cumsum(padded)
    starts = ends - padded
    pos = starts[flat_e] + rank
    order = jnp.argsort(pos).astype(jnp.int32)
    first_slot = jnp.cumsum(counts) - counts
    tile_expert = jnp.minimum(
        jnp.searchsorted(ends, jnp.arange(nt, dtype=jnp.int32) * tm, side="right"), n_exp - 1).astype(jnp.int32)
    row = jnp.arange(nt * tm, dtype=jnp.int32)
    row_e = tile_expert[row // tm]
    row_rank = row - starts[row_e]
    valid = row_rank < counts[row_e]
    slot = order[jnp.clip(first_slot[row_e] + row_rank, 0, slots - 1)]
    src_tok = jnp.where(valid, slot // TOP_K, 0).astype(jnp.int32)
    row_gate = jnp.where(valid, top_w.reshape(slots)[slot], 0.0)
    n_used = (ends[-1] // tm).astype(jnp.int32).reshape(1)
    return src_tok, row_gate, tile_expert, n_used, pos.astype(jnp.int32).reshape(t, TOP_K)


def _pack_halves(xb):
    half = xb.shape[1] // 2
    lo = lax.bitcast_convert_type(xb[:, :half], jnp.uint16).astype(jnp.uint32)
    hi = lax.bitcast_convert_type(xb[:, half:], jnp.uint16).astype(jnp.uint32)
    return lo | (hi << 16)


def _unpack_halves(u):
    lo = pltpu.bitcast(u << 16, _F32).astype(_BF16)
    hi = pltpu.bitcast(u & jnp.uint32(0xFFFF0000), _F32).astype(_BF16)
    return jnp.concatenate([lo, hi], axis=1)


def _gather_kernel(nu_ref, idx_ref, src_ref, o_ref, buf_ref, sem, *, tm):
    used = pl.program_id(0) < nu_ref[0]

    @pl.when(used)
    def _():
        def issue(r, carry):
            pltpu.make_async_copy(src_ref.at[pl.ds(idx_ref[0, r], 1)], buf_ref.at[pl.ds(r, 1)], sem).start()
            return carry

        lax.fori_loop(0, tm, issue, 0, unroll=8)
        pltpu.make_async_copy(src_ref.at[pl.ds(0, tm)], buf_ref, sem).wait()
        o_ref[...] = _unpack_halves(buf_ref[...])

    @pl.when(jnp.logical_not(used))
    def _():
        o_ref[...] = jnp.zeros_like(o_ref)


def _gather_rows(xb, idx, n_used, tm):
    p = idx.shape[0]
    d = xb.shape[1]
    src = _pack_halves(xb)
    return pl.pallas_call(
        functools.partial(_gather_kernel, tm=tm),
        grid_spec=pltpu.PrefetchScalarGridSpec(
            num_scalar_prefetch=1,
            grid=(p // tm,),
            in_specs=[
                pl.BlockSpec((None, 1, tm), lambda i, nu: (i, 0, 0), memory_space=pltpu.SMEM),
                pl.BlockSpec(memory_space=pl.ANY),
            ],
            out_specs=pl.BlockSpec((tm, d), lambda i, nu: (i, 0)),
            scratch_shapes=[pltpu.VMEM((tm, d // 2), jnp.uint32), pltpu.SemaphoreType.DMA(())],
        ),
        out_shape=jax.ShapeDtypeStruct((p, d), _BF16),
        compiler_params=_params("arbitrary"),
        name="gather_rows",
    )(n_used, idx.reshape(p // tm, 1, tm), src)


def _gup_kernel(te_ref, nu_ref, x_ref, w1_ref, w3_ref, o_ref, w1b_ref, w3b_ref):
    i = pl.program_id(1)
    new_expert = jnp.logical_or(i == 0, te_ref[i] != te_ref[jnp.maximum(i - 1, 0)])

    @pl.when(new_expert)
    def _():
        w1b_ref[...] = w1_ref[...].astype(_BF16)
        w3b_ref[...] = w3_ref[...].astype(_BF16)

    @pl.when(i < nu_ref[0])
    def _():
        x = x_ref[...]
        a = jnp.dot(x, w1b_ref[...], preferred_element_type=_F32)
        b = jnp.dot(x, w3b_ref[...], preferred_element_type=_F32)
        o_ref[...] = (a * jax.nn.sigmoid(a) * b).astype(o_ref.dtype)

    @pl.when(i >= nu_ref[0])
    def _():
        o_ref[...] = jnp.zeros_like(o_ref)


def _grouped_up(xs, w1, w3, tile_expert, n_used, tm, tn_target=512):
    p = xs.shape[0]
    _, d, f = w1.shape
    tn = _tile(f, tn_target, LANES)
    w_spec = pl.BlockSpec((None, d, tn), lambda j, i, te, nu: (te[i], 0, j))
    return pl.pallas_call(
        _gup_kernel,
        grid_spec=pltpu.PrefetchScalarGridSpec(
            num_scalar_prefetch=2,
            grid=(f // tn, p // tm),
            in_specs=[pl.BlockSpec((tm, xs.shape[1]), lambda j, i, te, nu: (i, 0)), w_spec, w_spec],
            out_specs=pl.BlockSpec((tm, tn), lambda j, i, te, nu: (i, j)),
            scratch_shapes=[pltpu.VMEM((d, tn), _BF16), pltpu.VMEM((d, tn), _BF16)],
        ),
        out_shape=jax.ShapeDtypeStruct((p, f), _BF16),
        compiler_params=_params("arbitrary", "arbitrary"),
        name="moe_up",
    )(tile_expert, n_used, xs, w1, w3)


def _gdown_kernel(te_ref, nu_ref, h_ref, w_ref, g_ref, o_ref, *, nk):
    i = pl.program_id(0)
    kk = pl.program_id(1)
    used = i < nu_ref[0]

    @pl.when(used)
    def _():
        part = jnp.dot(h_ref[...], w_ref[...].astype(_BF16), preferred_element_type=_F32)

        @pl.when(kk == 0)
        def _():
            o_ref[...] = part

        @pl.when(kk > 0)
        def _():
            o_ref[...] += part

        @pl.when(kk == nk - 1)
        def _():
            g = g_ref[...]
            o_ref[...] = o_ref[...] * jnp.concatenate([g] * (o_ref.shape[1] // g.shape[1]), axis=1)

    @pl.when(jnp.logical_and(jnp.logical_not(used), kk == nk - 1))
    def _():
        o_ref[...] = jnp.zeros_like(o_ref)


def _grouped_down(h, w2, row_gate, tile_expert, n_used, tm, tk_target=1024):
    p, f = h.shape
    d = w2.shape[2]
    tk = _tile(f, tk_target, LANES)
    nk = f // tk
    gl = LANES if d % LANES == 0 else d
    gate = jnp.broadcast_to(row_gate[:, None], (p, gl))
    return pl.pallas_call(
        functools.partial(_gdown_kernel, nk=nk),
        grid_spec=pltpu.PrefetchScalarGridSpec(
            num_scalar_prefetch=2,
            grid=(p // tm, nk),
            in_specs=[
                pl.BlockSpec((tm, tk), lambda i, kk, te, nu: (i, kk)),
                pl.BlockSpec((None, tk, d), lambda i, kk, te, nu: (te[i], kk, 0)),
                pl.BlockSpec((tm, gl), lambda i, kk, te, nu: (i, 0)),
            ],
            out_specs=pl.BlockSpec((tm, d), lambda i, kk, te, nu: (i, 0)),
        ),
        out_shape=jax.ShapeDtypeStruct((p, d), _F32),
        compiler_params=_params("arbitrary", "arbitrary"),
        name="moe_down",
    )(tile_expert, n_used, h, w2, gate)


def _combine_ln_kernel(pos_ref, ys_ref, res_ref, lw_ref, lb_ref, o_ref, ob_ref, buf_ref, sem, *, alpha, tc):
    def issue(r, carry):
        for j in range(TOP_K):
            pltpu.make_async_copy(
                ys_ref.at[pl.ds(pos_ref[0, r * TOP_K + j], 1)], buf_ref.at[j, pl.ds(r, 1)], sem.at[j]).start()
        return carry

    lax.fori_loop(0, tc, issue, 0, unroll=4)
    y = alpha * res_ref[...]
    for j in range(TOP_K):
        pltpu.make_async_copy(ys_ref.at[pl.ds(0, tc)], buf_ref.at[j], sem.at[j]).wait()
        y = y + buf_ref[j]
    mu = jnp.mean(y, axis=-1, keepdims=True)
    yc = y - mu
    var = jnp.mean(yc * yc, axis=-1, keepdims=True)
    out = yc * lax.rsqrt(var + LN_EPS) * lw_ref[...] + lb_ref[...]
    o_ref[...] = out
    ob_ref[...] = out.astype(_BF16)


def _combine_ln(ys, pos, res, ln_w, ln_b, alpha, tc_target=256):
    t, d = res.shape
    tc = _tile(t, tc_target, 8)
    return pl.pallas_call(
        functools.partial(_combine_ln_kernel, alpha=alpha, tc=tc),
        grid=(t // tc,),
        in_specs=[
            pl.BlockSpec((None, 1, tc * TOP_K), lambda i: (i, 0, 0), memory_space=pltpu.SMEM),
            pl.BlockSpec(memory_space=pl.ANY),
            pl.BlockSpec((tc, d), lambda i: (i, 0)),
            pl.BlockSpec((1, d), lambda i: (0, 0)),
            pl.BlockSpec((1, d), lambda i: (0, 0)),
        ],
        out_specs=[pl.BlockSpec((tc, d), lambda i: (i, 0)), pl.BlockSpec((tc, d), lambda i: (i, 0))],
        out_shape=[jax.ShapeDtypeStruct((t, d), _F32), jax.ShapeDtypeStruct((t, d), _BF16)],
        scratch_shapes=[pltpu.VMEM((TOP_K, tc, d), _F32), pltpu.SemaphoreType.DMA((TOP_K,))],
        compiler_params=_params("arbitrary"),
        name="combine_ln",
    )(pos.reshape(t // tc, 1, tc * TOP_K), ys, res, ln_w.reshape(1, d), ln_b.reshape(1, d))


def _moe_ln(xf, xb, w_router, w1, w3, w2, ln_w, ln_b, alpha, tm_target=1024):
    t = xf.shape[0]
    n_exp = w_router.shape[1]
    tm = _tile(TOP_K * t, tm_target, 8)
    top_w, top_e = _router_top2(xf, w_router)
    src_tok, row_gate, tile_expert, n_used, pos = _route(top_w, top_e, n_exp, tm)
    xs = _gather_rows(xb, src_tok, n_used, tm)
    h = _grouped_up(xs, w1, w3, tile_expert, n_used, tm)
    ys = _grouped_down(h, w2, row_gate, tile_expert, n_used, tm)
    return _combine_ln(ys, pos, xf, ln_w, ln_b, alpha)


def _rwkv_kernel(r_ref, w_ref, k_ref, v_ref, an_ref, b_ref, y_ref, s_ref, sa_ref, *, tt, n, sub):
    @pl.when(pl.program_id(0) == 0)
    def _():
        s_ref[...] = jnp.zeros_like(s_ref)
        sa_ref[...] = jnp.zeros_like(sa_ref)

    nblk = n // sub
    zero = jnp.zeros(v_ref.shape[1:], _F32)

    def step(t, sa):
        v_t = v_ref[t]

        def upd_body(j, carry):
            y, sa_next = carry
            rows = pl.ds(pl.multiple_of(j * sub, sub), sub)
            w_blk = w_ref[t, rows, :]
            b_blk = b_ref[t, rows, :]
            k_blk = k_ref[t, rows, :]
            r_blk = r_ref[t, rows, :]
            a_blk = an_ref[t, rows, :]
            for u in range(sub):
                s_new = s_ref[j * sub + u] * w_blk[u:u + 1, :] + sa * b_blk[u:u + 1, :] + v_t * k_blk[u:u + 1, :]
                s_ref[j * sub + u] = s_new
                y = y + s_new * r_blk[u:u + 1, :]
                sa_next = sa_next + s_new * a_blk[u:u + 1, :]
            return y, sa_next

        y, sa_next = lax.fori_loop(0, nblk, upd_body, (zero, zero))
        y_ref[t] = y
        return sa_next

    sa_ref[...] = lax.fori_loop(0, tt, step, sa_ref[...])


def _rwkv_scan(r, w, k, v, a, b, tt_target=32):
    s, n, l = r.shape
    tt = _tile(s, tt_target, 1)
    sub = 8 if n % 8 == 0 else 1
    a = jnp.concatenate([a[1:], jnp.zeros_like(a[:1])], axis=0)
    spec = pl.BlockSpec((tt, n, l), lambda i: (i, 0, 0))
    return pl.pallas_call(
        functools.partial(_rwkv_kernel, tt=tt, n=n, sub=sub),
        grid=(s // tt,),
        in_specs=[spec] * 6,
        out_specs=spec,
        out_shape=jax.ShapeDtypeStruct((s, n, l), _F32),
        scratch_shapes=[pltpu.VMEM((n, n, l), _F32), pltpu.VMEM((n, l), _F32)],
        compiler_params=_params("arbitrary"),
        name="rwkv7_scan",
    )(r, w, k, v, a, b)


HGRN_CHUNK = 64
HGRN_SUB = 16


def _dot_nt(a, b):
    return lax.dot_general(a, b, (((1,), (1,)), ((), ())), preferred_element_type=_F32)


def _dot_tn(a, b):
    return lax.dot_general(a, b, (((0,), (0,)), ((), ())), preferred_element_type=_F32)


def _hgrn_chunk_kernel(q_ref, hf_ref, hi_ref, hg_ref, lb_ref, nw_ref, o_ref, st_ref, *, first_layer, chunk, sub):
    @pl.when(pl.program_id(2) == 0)
    def _():
        st_ref[...] = jnp.zeros_like(st_ref)

    ts, dk = q_ref.shape
    nsub = chunk // sub
    lb = lb_ref[...]
    tri = (lax.broadcasted_iota(jnp.int32, (chunk, chunk), 0)
           >= lax.broadcasted_iota(jnp.int32, (chunk, chunk), 1)).astype(_F32)
    local_row = lax.broadcasted_iota(jnp.int32, (chunk, 1), 0) % sub

    def rows_of_subchunks(t, s):
        return jnp.concatenate(
            [jnp.broadcast_to(t[n * sub + s:n * sub + s + 1, :], (sub, t.shape[1])) for n in range(nsub)], axis=0)

    for c in range(ts // chunk):
        rows = slice(c * chunk, (c + 1) * chunk)
        q, hf, hi = q_ref[rows, :], hf_ref[rows, :], hi_ref[rows, :]
        if first_layer:
            log_f = jax.nn.log_sigmoid(hf)
        else:
            log_f = jnp.log(lb + (1.0 - lb) * jax.nn.sigmoid(hf))
        kin = (1.0 - lb) * jax.nn.sigmoid(-hf)
        cum = jnp.dot(tri, log_f, preferred_element_type=_F32, precision=lax.Precision.HIGHEST)
        total = cum[chunk - 1:, :]
        hi_b = hi.astype(_BF16)

        st = st_ref[...]
        o = _dot_nt((q * jnp.exp(cum)).astype(_BF16), st.astype(_BF16))
        k_end = (kin * jnp.exp(total - cum)).astype(_BF16)
        st_ref[...] = st * jnp.exp(total) + _dot_tn(hi_b, k_end)

        parts = [jnp.zeros((sub, hi.shape[1]), _F32)]
        for n in range(1, nsub):
            start = cum[n * sub - 1:n * sub, :]
            q_n = q[n * sub:(n + 1) * sub, :] * jnp.exp(cum[n * sub:(n + 1) * sub, :] - start)
            k_n = kin[:n * sub, :] * jnp.exp(start - cum[:n * sub, :])
            a_n = _dot_nt(q_n.astype(_BF16), k_n.astype(_BF16))
            parts.append(jnp.dot(a_n.astype(_BF16), hi_b[:n * sub, :], preferred_element_type=_F32))
        o = o + jnp.concatenate(parts, axis=0)

        for s in range(sub):
            diff = cum - rows_of_subchunks(cum, s)
            decay = jnp.exp(jnp.where(local_row >= s, diff, -jnp.inf))
            score = jnp.sum(q * rows_of_subchunks(kin, s) * decay, axis=1, keepdims=True)
            o = o + score * rows_of_subchunks(hi, s)

        hg = hg_ref[rows, :]
        on = o * lax.rsqrt(jnp.mean(o * o, axis=-1, keepdims=True) + RMS_EPS) * nw_ref[...]
        o_ref[rows, :] = (on * hg * jax.nn.sigmoid(hg)).astype(o_ref.dtype)


def _hgrn_chunked(p, lb, norm_w, bsz, hw, h_dim, first_layer, ts_target=512):
    m = p.shape[0]
    seq = m // bsz
    heads = hw // h_dim
    chunk = min(HGRN_CHUNK, seq)
    ts = _tile(seq, ts_target, chunk)
    nst = seq // ts
    col = lambda sec: pl.BlockSpec((ts, h_dim), lambda b, h, s: (b * nst + s, sec * heads + h))
    return pl.pallas_call(
        functools.partial(_hgrn_chunk_kernel, first_layer=first_layer, chunk=chunk, sub=min(HGRN_SUB, chunk)),
        grid=(bsz, heads, nst),
        in_specs=[col(0), col(1), col(2), col(3),
                  pl.BlockSpec((1, h_dim), lambda b, h, s: (0, h)),
                  pl.BlockSpec((1, h_dim), lambda b, h, s: (0, 0))],
        out_specs=pl.BlockSpec((ts, h_dim), lambda b, h, s: (b * nst + s, h)),
        out_shape=jax.ShapeDtypeStruct((m, hw), _BF16),
        scratch_shapes=[pltpu.VMEM((h_dim, h_dim), _F32)],
        compiler_params=_params("arbitrary", "arbitrary", "arbitrary"),
        name="hgrn2_chunked",
    )(p, p, p, p, lb.reshape(1, hw), norm_w.reshape(1, h_dim))


def _group_matrix(width, group):
    col = jnp.arange(width, dtype=jnp.int32)[:, None] // group
    return (col == jnp.arange(width // group, dtype=jnp.int32)[None, :]).astype(_BF16)


def _gsum(x, g):
    hi = x.astype(_BF16)
    lo = (x - hi.astype(_F32)).astype(_BF16)
    return jnp.dot(hi, g, preferred_element_type=_F32) + jnp.dot(lo, g, preferred_element_type=_F32)


def _prep_kernel(*refs, first_layer, ranks):
    if first_layer:
        (k_ref, t_ref, vec_ref, wt_ref, g_ref, gt_ref,
         w_out, k_out, a_out, b_out, gate_out) = refs
    else:
        (k_ref, t_ref, v_ref, vf_ref, vec_ref, wt_ref, g_ref, gt_ref,
         w_out, k_out, a_out, b_out, gate_out, v_out) = refs
    width = k_ref.shape[1]
    d_rank, a_rank, g_rank = ranks

    t = t_ref[...]
    col = lax.broadcasted_iota(jnp.int32, t.shape, 1)
    act = jnp.where(col < d_rank, jnp.tanh(t),
                    jnp.where(jnp.logical_and(col >= d_rank + a_rank, col < d_rank + a_rank + g_rank),
                              jax.nn.sigmoid(t), t))
    low = jnp.dot(act.astype(_BF16), wt_ref[...], preferred_element_type=_F32)
    w0, a0, v0, k_k, k_a = (vec_ref[n:n + 1, :] for n in range(5))
    w_log = -jax.nn.softplus(-(w0 + low[:, :width])) - 0.5
    w_out[...] = jnp.exp(-jnp.exp(w_log))
    iclr = jax.nn.sigmoid(a0 + low[:, width:2 * width])
    gate_out[...] = low[:, 2 * width:3 * width]
    if not first_layer:
        v = v_ref[...]
        v_out[...] = v + (vf_ref[...] - v) * jax.nn.sigmoid(v0 + low[:, 3 * width:])

    k = k_ref[...]
    kk = k * k_k
    inv = 1.0 / jnp.maximum(jnp.sqrt(_gsum(kk * kk, g_ref[...])), L2_EPS)
    kk = kk * _gsum(inv, gt_ref[...])
    k_out[...] = k * (1.0 + (iclr - 1.0) * k_a)
    a_out[...] = -kk
    b_out[...] = kk * iclr


def _mixer_prep(p, tail, p_first, vecs, w_tail, ranks, hw, rw, r_n, first_layer, tm_target=256):
    m = p.shape[0]
    tm = _tile(m, tm_target, 8)
    hb = 4 * hw // rw
    blk = lambda c: pl.BlockSpec((tm, rw), lambda i: (i, c))
    full = lambda a: pl.BlockSpec(a.shape, lambda i: (0,) * a.ndim)
    g = _group_matrix(rw, r_n)
    gt = g.T
    n_out = 5 if first_layer else 6
    in_specs = [blk(hb + 1), pl.BlockSpec((tm, tail.shape[1]), lambda i: (i, 0))]
    args = [p, tail]
    if not first_layer:
        in_specs += [blk(hb + 2), blk(hb + 2)]
        args += [p, p_first]
    consts = [vecs, w_tail, g, gt]
    in_specs += [full(a) for a in consts]
    return pl.pallas_call(
        functools.partial(_prep_kernel, first_layer=first_layer, ranks=ranks),
        grid=(m // tm,),
        in_specs=in_specs,
        out_specs=[pl.BlockSpec((tm, rw), lambda i: (i, 0))] * n_out,
        out_shape=[jax.ShapeDtypeStruct((m, rw), _F32)] * n_out,
        compiler_params=_params("arbitrary"),
        name="mixer_prep",
    )(*args, *consts)


def _post_kernel(y_ref, r_ref, k_ref, v_ref, gate_ref, vec_ref, gr_ref, grt_ref, o_ref, *, r_dim):
    lnx_w, lnx_b, r_k = (vec_ref[n:n + 1, :] for n in range(3))
    y = y_ref[...]
    gr = gr_ref[...]
    grt = grt_ref[...]
    yc = y - _gsum(_gsum(y, gr) * (1.0 / r_dim), grt)
    rs_y = lax.rsqrt(_gsum(yc * yc, gr) * (1.0 / r_dim) + GN_EPS)
    bonus = _gsum(_gsum(r_ref[...] * k_ref[...] * r_k, gr), grt) * v_ref[...]
    o_ref[...] = ((yc * _gsum(rs_y, grt) * lnx_w + lnx_b + bonus) * gate_ref[...]).astype(o_ref.dtype)


def _mixer_post(y, p, k, v_src, v_blk, gate, vecs, hw, rw, r_dim, tm_target=512):
    m = y.shape[0]
    tm = _tile(m, tm_target, 8)
    hb = 4 * hw // rw
    row = lambda c=0: pl.BlockSpec((tm, rw), lambda i: (i, c))
    full = lambda a: pl.BlockSpec(a.shape, lambda i: (0,) * a.ndim)
    gr = _group_matrix(rw, r_dim)
    consts = [vecs, gr, gr.T]
    return pl.pallas_call(
        functools.partial(_post_kernel, r_dim=r_dim),
        grid=(m // tm,),
        in_specs=[row(), row(hb), row(), row(v_blk), row()] + [full(a) for a in consts],
        out_specs=row(),
        out_shape=jax.ShapeDtypeStruct((m, rw), _BF16),
        compiler_params=_params("arbitrary"),
        name="mixer_post",
    )(y, p, k, v_src, gate, *consts)


def _to_scan(t, bsz, heads):
    m, width = t.shape
    seq, n = m // bsz, width // heads
    return t.reshape(bsz, seq, heads, n).transpose(1, 3, 0, 2).reshape(seq, n, bsz * heads)


def _from_scan(t, bsz, heads):
    seq, n, _ = t.shape
    return t.reshape(seq, n, bsz, heads).transpose(2, 0, 3, 1).reshape(bsz * seq, heads * n)


def kernel(x, w_in, w_in_vres, hgrn_lower_bounds, hgrn_norm_w, rwkv_mu, rwkv_mu_vres, rwkv_w0, rwkv_w2, rwkv_a0, rwkv_a2, rwkv_v0, rwkv_v2, rwkv_g2, rwkv_k_k, rwkv_k_a, rwkv_r_k, rwkv_lnx_w, rwkv_lnx_b, w_out, ln1_w, ln1_b, ffn_w1, ffn_w3, ffn_w2, moe_router, moe_w1, moe_w3, moe_w2, ln2_w, ln2_b):
    bsz, seq, d = x.shape
    depth = w_in.shape[0]
    alpha = (2 * depth) ** 0.25
    hw = hgrn_lower_bounds.shape[1]
    h_dim = hgrn_norm_w.shape[1]
    rw = rwkv_w0.shape[1]
    r_heads, r_dim = rwkv_r_k.shape[1], rwkv_r_k.shape[2]
    assert hw == rw and hw % LANES == 0
    ranks = (rwkv_w2.shape[1], rwkv_a2.shape[1], rwkv_g2.shape[1])
    v_rank = rwkv_v2.shape[1]
    main = 4 * hw + 3 * rw
    tail_w = -(-(sum(ranks) + v_rank) // LANES) * LANES
    hb = 4 * hw // rw
    m = bsz * seq

    lb_all = jax.nn.softmax(hgrn_lower_bounds.astype(_F32), axis=0)
    lb_all = jnp.cumsum(lb_all, axis=0) - lb_all[0:1]

    xf = x.reshape(m, d)
    xb = xf.astype(_BF16)
    p_first = None
    for layer in range(depth):
        first = layer == 0
        w_tail_in, mu = w_in[layer][:, main:], rwkv_mu[layer]
        if not first:
            w_tail_in = jnp.concatenate([w_tail_in, w_in_vres[layer - 1]], axis=1)
            mu = jnp.concatenate([mu, rwkv_mu_vres[layer - 1]], axis=0)
        pad = tail_w - w_tail_in.shape[1]
        w_tail_in = jnp.pad(w_tail_in, ((0, 0), (0, pad)))
        mu = jnp.concatenate([jnp.zeros((4 * hw,), _F32), mu.astype(_F32), jnp.zeros((pad,), _F32)])
        p = _in_proj(xb, w_in[layer][:, :main].astype(_BF16), mu[:main], seq, 4 * hw)
        tail = _in_proj(xb, w_tail_in.astype(_BF16), mu[main:], seq, 0)

        blocks = [rwkv_w2[layer], rwkv_a2[layer], rwkv_g2[layer]]
        if not first:
            blocks.append(rwkv_v2[layer - 1])
        w_tail_out = jnp.zeros((tail_w, 4 * rw), _F32)
        r0 = 0
        for n, blk in enumerate(blocks):
            w_tail_out = w_tail_out.at[r0:r0 + blk.shape[0], n * rw:(n + 1) * rw].set(blk)
            r0 += blk.shape[0]
        v0 = jnp.zeros((rw,), _F32) if first else rwkv_v0[layer - 1]
        vecs = jnp.stack([rwkv_w0[layer], rwkv_a0[layer], v0, rwkv_k_k[layer], rwkv_k_a[layer]]).astype(_F32)
        vecs = jnp.pad(vecs, ((0, 3), (0, 0)))
        outs = _mixer_prep(p, tail, p_first, vecs, w_tail_out.astype(_BF16), ranks, hw, rw, r_dim, first)
        if first:
            decay, k, a, b, gate = outs
            p_first = p
            v_src, v_blk = p, hb + 2
            v = p[:, (hb + 2) * rw:(hb + 3) * rw]
        else:
            decay, k, a, b, gate, v = outs
            v_src, v_blk = v, 0

        o_h = _hgrn_chunked(p, lb_all[layer], hgrn_norm_w[layer].astype(_F32), bsz, hw, h_dim, first)
        r = p[:, hb * rw:(hb + 1) * rw]
        y = _rwkv_scan(*[_to_scan(t, bsz, r_heads) for t in (r, decay, k, v, a, b)])
        y = _from_scan(y, bsz, r_heads)

        post_vecs = jnp.stack([rwkv_lnx_w[layer], rwkv_lnx_b[layer], rwkv_r_k[layer].reshape(rw)]).astype(_F32)
        post_vecs = jnp.pad(post_vecs, ((0, 5), (0, 0)))
        o_r = _mixer_post(y, p, k, v_src, v_blk, gate, post_vecs, hw, rw, r_dim)
        mix = jnp.concatenate([o_h, o_r], axis=1)
        xf, xb = _down_ln(mix, w_out[layer].astype(_BF16), xf, ln1_w[layer], ln1_b[layer], alpha)

        j = layer // 2
        if layer % 2 == 0:
            h = _swiglu_up(xb, ffn_w1[j].astype(_BF16), ffn_w3[j].astype(_BF16))
            xf, xb = _down_ln(h, ffn_w2[j].astype(_BF16), xf, ln2_w[layer], ln2_b[layer], alpha)
        else:
            xf, xb = _moe_ln(xf, xb, moe_router[j], moe_w1[j], moe_w3[j], moe_w2[j], ln2_w[layer], ln2_b[layer], alpha)
    return xf.reshape(bsz, seq, d)
```

```python
import functools
import math

import jax
import jax.numpy as jnp
from jax import lax
from jax.experimental import pallas as pl
from jax.experimental.pallas import tpu as pltpu

LN_EPS = 1e-5
RMS_EPS = 1e-5
GN_EPS = 64e-5
L2_EPS = 1e-12
TOP_K = 2

LANES = 128
VMEM_LIMIT_BYTES = 56 * 1024 * 1024

_F32 = jnp.float32
_BF16 = jnp.bfloat16


def _tile(n, target, align):
    best = None
    for t in range(align, min(n, target) + 1, align):
        if n % t == 0:
            best = t
    return best if best is not None else n


def _params(*sem):
    return pltpu.CompilerParams(dimension_semantics=sem, vmem_limit_bytes=VMEM_LIMIT_BYTES)


def _in_proj_kernel(x_ref, w_ref, mu_ref, o_ref, last_ref, *, shift_from, tiles_per_seq):
    j = pl.program_id(0)
    i = pl.program_id(1)
    acc = jnp.dot(x_ref[...], w_ref[...], preferred_element_type=_F32)

    @pl.when(j < shift_from)
    def _():
        o_ref[...] = acc

    @pl.when(j >= shift_from)
    def _():
        carry = jnp.where(i % tiles_per_seq == 0, 0.0, last_ref[...])
        row = lax.broadcasted_iota(jnp.int32, acc.shape, 0)
        prev = jnp.where(row == 0, carry, pltpu.roll(acc, 1, axis=0))
        last_ref[...] = acc[acc.shape[0] - 1:, :]
        o_ref[...] = acc + mu_ref[...] * (prev - acc)


def _in_proj(x, w, mu, seq, shift_from_col, tm_target=1024, tn_target=1024):
    m, k = x.shape
    n = w.shape[1]
    tm = _tile(seq, tm_target, 8)
    tn = _tile(math.gcd(n, shift_from_col) if shift_from_col else n, tn_target, LANES)
    return pl.pallas_call(
        functools.partial(_in_proj_kernel, shift_from=shift_from_col // tn, tiles_per_seq=seq // tm),
        grid=(n // tn, m // tm),
        in_specs=[
            pl.BlockSpec((tm, k), lambda j, i: (i, 0)),
            pl.BlockSpec((k, tn), lambda j, i: (0, j)),
            pl.BlockSpec((1, tn), lambda j, i: (0, j)),
        ],
        out_specs=pl.BlockSpec((tm, tn), lambda j, i: (i, j)),
        out_shape=jax.ShapeDtypeStruct((m, n), _F32),
        scratch_shapes=[pltpu.VMEM((1, tn), _F32)],
        compiler_params=_params("arbitrary", "arbitrary"),
        name="in_proj",
    )(x, w, mu.reshape(1, n).astype(_F32))


def _up_kernel(x_ref, w1_ref, w3_ref, o_ref):
    x = x_ref[...]
    a = jnp.dot(x, w1_ref[...], preferred_element_type=_F32)
    b = jnp.dot(x, w3_ref[...], preferred_element_type=_F32)
    o_ref[...] = (a * jax.nn.sigmoid(a) * b).astype(o_ref.dtype)


def _swiglu_up(x, w1, w3, tm_target=1024, tn_target=512):
    m, k = x.shape
    f = w1.shape[1]
    tm = _tile(m, tm_target, 8)
    tn = _tile(f, tn_target, LANES)
    w_spec = pl.BlockSpec((k, tn), lambda j, i: (0, j))
    return pl.pallas_call(
        _up_kernel,
        grid=(f // tn, m // tm),
        in_specs=[pl.BlockSpec((tm, k), lambda j, i: (i, 0)), w_spec, w_spec],
        out_specs=pl.BlockSpec((tm, tn), lambda j, i: (i, j)),
        out_shape=jax.ShapeDtypeStruct((m, f), _BF16),
        compiler_params=_params("arbitrary", "arbitrary"),
        name="swiglu_up",
    )(x, w1, w3)


def _down_ln_kernel(h_ref, w_ref, res_ref, lw_ref, lb_ref, o_ref, ob_ref, *, alpha, nk):
    kk = pl.program_id(1)
    part = jnp.dot(h_ref[...], w_ref[...], preferred_element_type=_F32)

    @pl.when(kk == 0)
    def _():
        o_ref[...] = part

    @pl.when(kk > 0)
    def _():
        o_ref[...] += part

    @pl.when(kk == nk - 1)
    def _():
        y = alpha * res_ref[...] + o_ref[...]
        mu = jnp.mean(y, axis=-1, keepdims=True)
        yc = y - mu
        var = jnp.mean(yc * yc, axis=-1, keepdims=True)
        out = yc * lax.rsqrt(var + LN_EPS) * lw_ref[...] + lb_ref[...]
        o_ref[...] = out
        ob_ref[...] = out.astype(_BF16)


def _down_ln(h, w, res, ln_w, ln_b, alpha, tm_target=512, tk_target=2048):
    m = h.shape[0]
    f, d = w.shape
    tm = _tile(m, tm_target, 8)
    tk = _tile(f, tk_target, LANES)
    nk = f // tk
    return pl.pallas_call(
        functools.partial(_down_ln_kernel, alpha=alpha, nk=nk),
        grid=(m // tm, nk),
        in_specs=[
            pl.BlockSpec((tm, tk), lambda i, kk: (i, kk)),
            pl.BlockSpec((tk, d), lambda i, kk: (kk, 0)),
            pl.BlockSpec((tm, d), lambda i, kk: (i, 0)),
            pl.BlockSpec((1, d), lambda i, kk: (0, 0)),
            pl.BlockSpec((1, d), lambda i, kk: (0, 0)),
        ],
        out_specs=[
            pl.BlockSpec((tm, d), lambda i, kk: (i, 0)),
            pl.BlockSpec((tm, d), lambda i, kk: (i, 0)),
        ],
        out_shape=[jax.ShapeDtypeStruct((m, d), _F32), jax.ShapeDtypeStruct((m, d), _BF16)],
        compiler_params=_params("arbitrary", "arbitrary"),
        name="down_ln",
    )(h, w, res, ln_w.reshape(1, d), ln_b.reshape(1, d))


def _router_kernel(x_ref, w_ref, g_ref, e_ref):
    logits = jnp.dot(x_ref[...], w_ref[...], preferred_element_type=_F32, precision=lax.Precision.HIGHEST)
    n_exp = logits.shape[1]
    col = lax.broadcasted_iota(jnp.int32, logits.shape, 1)
    m1 = jnp.max(logits, axis=1, keepdims=True)
    i1 = jnp.min(jnp.where(logits == m1, col, n_exp), axis=1, keepdims=True)
    rest = jnp.where(col == i1, -jnp.inf, logits)
    m2 = jnp.max(rest, axis=1, keepdims=True)
    i2 = jnp.min(jnp.where(rest == m2, col, n_exp), axis=1, keepdims=True)
    ex = jnp.exp(m2 - m1)
    den = 1.0 + ex
    two = lax.broadcasted_iota(jnp.int32, g_ref.shape, 1)
    g_ref[...] = jnp.where(two == 0, 1.0 / den, ex / den)
    e_ref[...] = jnp.where(two == 0, i1, i2)


def _router_top2(x, w_router, tm_target=512):
    m, d = x.shape
    e = w_router.shape[1]
    tm = _tile(m, tm_target, 8)
    return pl.pallas_call(
        _router_kernel,
        grid=(m // tm,),
        in_specs=[pl.BlockSpec((tm, d), lambda i: (i, 0)), pl.BlockSpec((d, e), lambda i: (0, 0))],
        out_specs=[pl.BlockSpec((tm, TOP_K), lambda i: (i, 0)), pl.BlockSpec((tm, TOP_K), lambda i: (i, 0))],
        out_shape=[jax.ShapeDtypeStruct((m, TOP_K), _F32), jax.ShapeDtypeStruct((m, TOP_K), jnp.int32)],
        compiler_params=_params("arbitrary"),
        name="router_top2",
    )(x, w_router)


def _route(top_w, top_e, n_exp, tm):
    t = top_e.shape[0]
    slots = TOP_K * t
    nt = slots // tm + n_exp
    flat_e = top_e.reshape(slots)
    onehot = (flat_e[:, None] == jnp.arange(n_exp, dtype=jnp.int32)[None, :]).astype(jnp.int32)
    counts = jnp.sum(onehot, axis=0)
    rank = jnp.sum((jnp.cumsum(onehot, axis=0) - onehot) * onehot, axis=1)
    padded = ((counts + tm - 1) // tm) * tm
    ends = jnp.cumsum(padded)
    starts = ends - padded
    pos = starts[flat_e] + rank
    order = jnp.argsort(pos).astype(jnp.int32)
    first_slot = jnp.cumsum(counts) - counts
    tile_expert = jnp.minimum(
        jnp.searchsorted(ends, jnp.arange(nt, dtype=jnp.int32) * tm, side="right"), n_exp - 1).astype(jnp.int32)
    row = jnp.arange(nt * tm, dtype=jnp.int32)
    row_e = tile_expert[row // tm]
    row_rank = row - starts[row_e]
    valid = row_rank < counts[row_e]
    slot = order[jnp.clip(first_slot[row_e] + row_rank, 0, slots - 1)]
    src_tok = jnp.where(valid, slot // TOP_K, 0).astype(jnp.int32)
    row_gate = jnp.where(valid, top_w.reshape(slots)[slot], 0.0)
    n_used = (ends[-1] // tm).astype(jnp.int32).reshape(1)
    return src_tok, row_gate, tile_expert, n_used, pos.astype(jnp.int32).reshape(t, TOP_K)


def _pack_halves(xb):
    half = xb.shape[1] // 2
    lo = lax.bitcast_convert_type(xb[:, :half], jnp.uint16).astype(jnp.uint32)
    hi = lax.bitcast_convert_type(xb[:, half:], jnp.uint16).astype(jnp.uint32)
    return lo | (hi << 16)


def _unpack_halves(u):
    lo = pltpu.bitcast(u << 16, _F32).astype(_BF16)
    hi = pltpu.bitcast(u & jnp.uint32(0xFFFF0000), _F32).astype(_BF16)
    return jnp.concatenate([lo, hi], axis=1)


def _gather_kernel(nu_ref, idx_ref, src_ref, o_ref, buf_ref, sem, *, tm):
    used = pl.program_id(0) < nu_ref[0]

    @pl.when(used)
    def _():
        def issue(r, carry):
            pltpu.make_async_copy(src_ref.at[pl.ds(idx_ref[0, r], 1)], buf_ref.at[pl.ds(r, 1)], sem).start()
            return carry

        lax.fori_loop(0, tm, issue, 0, unroll=8)
        pltpu.make_async_copy(src_ref.at[pl.ds(0, tm)], buf_ref, sem).wait()
        o_ref[...] = _unpack_halves(buf_ref[...])

    @pl.when(jnp.logical_not(used))
    def _():
        o_ref[...] = jnp.zeros_like(o_ref)


def _gather_rows(xb, idx, n_used, tm):
    p = idx.shape[0]
    d = xb.shape[1]
    src = _pack_halves(xb)
    return pl.pallas_call(
        functools.partial(_gather_kernel, tm=tm),
        grid_spec=pltpu.PrefetchScalarGridSpec(
            num_scalar_prefetch=1,
            grid=(p // tm,),
            in_specs=[
                pl.BlockSpec((None, 1, tm), lambda i, nu: (i, 0, 0), memory_space=pltpu.SMEM),
                pl.BlockSpec(memory_space=pl.ANY),
            ],
            out_specs=pl.BlockSpec((tm, d), lambda i, nu: (i, 0)),
            scratch_shapes=[pltpu.VMEM((tm, d // 2), jnp.uint32), pltpu.SemaphoreType.DMA(())],
        ),
        out_shape=jax.ShapeDtypeStruct((p, d), _BF16),
        compiler_params=_params("arbitrary"),
        name="gather_rows",
    )(n_used, idx.reshape(p // tm, 1, tm), src)


def _gup_kernel(te_ref, nu_ref, x_ref, w1_ref, w3_ref, o_ref, w1b_ref, w3b_ref):
    i = pl.program_id(1)
    new_expert = jnp.logical_or(i == 0, te_ref[i] != te_ref[jnp.maximum(i - 1, 0)])

    @pl.when(new_expert)
    def _():
        w1b_ref[...] = w1_ref[...].astype(_BF16)
        w3b_ref[...] = w3_ref[...].astype(_BF16)

    @pl.when(i < nu_ref[0])
    def _():
        x = x_ref[...]
        a = jnp.dot(x, w1b_ref[...], preferred_element_type=_F32)
        b = jnp.dot(x, w3b_ref[...], preferred_element_type=_F32)
        o_ref[...] = (a * jax.nn.sigmoid(a) * b).astype(o_ref.dtype)

    @pl.when(i >= nu_ref[0])
    def _():
        o_ref[...] = jnp.zeros_like(o_ref)


def _grouped_up(xs, w1, w3, tile_expert, n_used, tm, tn_target=512):
    p = xs.shape[0]
    _, d, f = w1.shape
    tn = _tile(f, tn_target, LANES)
    w_spec = pl.BlockSpec((None, d, tn), lambda j, i, te, nu: (te[i], 0, j))
    return pl.pallas_call(
        _gup_kernel,
        grid_spec=pltpu.PrefetchScalarGridSpec(
            num_scalar_prefetch=2,
            grid=(f // tn, p // tm),
            in_specs=[pl.BlockSpec((tm, xs.shape[1]), lambda j, i, te, nu: (i, 0)), w_spec, w_spec],
            out_specs=pl.BlockSpec((tm, tn), lambda j, i, te, nu: (i, j)),
            scratch_shapes=[pltpu.VMEM((d, tn), _BF16), pltpu.VMEM((d, tn), _BF16)],
        ),
        out_shape=jax.ShapeDtypeStruct((p, f), _BF16),
        compiler_params=_params("arbitrary", "arbitrary"),
        name="moe_up",
    )(tile_expert, n_used, xs, w1, w3)


def _gdown_kernel(te_ref, nu_ref, h_ref, w_ref, g_ref, o_ref, *, nk):
    i = pl.program_id(0)
    kk = pl.program_id(1)
    used = i < nu_ref[0]

    @pl.when(used)
    def _():
        part = jnp.dot(h_ref[...], w_ref[...].astype(_BF16), preferred_element_type=_F32)

        @pl.when(kk == 0)
        def _():
            o_ref[...] = part

        @pl.when(kk > 0)
        def _():
            o_ref[...] += part

        @pl.when(kk == nk - 1)
        def _():
            g = g_ref[...]
            o_ref[...] = o_ref[...] * jnp.concatenate([g] * (o_ref.shape[1] // g.shape[1]), axis=1)

    @pl.when(jnp.logical_and(jnp.logical_not(used), kk == nk - 1))
    def _():
        o_ref[...] = jnp.zeros_like(o_ref)


def _grouped_down(h, w2, row_gate, tile_expert, n_used, tm, tk_target=1024):
    p, f = h.shape
    d = w2.shape[2]
    tk = _tile(f, tk_target, LANES)
    nk = f // tk
    gl = LANES if d % LANES == 0 else d
    gate = jnp.broadcast_to(row_gate[:, None], (p, gl))
    return pl.pallas_call(
        functools.partial(_gdown_kernel, nk=nk),
        grid_spec=pltpu.PrefetchScalarGridSpec(
            num_scalar_prefetch=2,
            grid=(p // tm, nk),
            in_specs=[
                pl.BlockSpec((tm, tk), lambda i, kk, te, nu: (i, kk)),
                pl.BlockSpec((None, tk, d), lambda i, kk, te, nu: (te[i], kk, 0)),
                pl.BlockSpec((tm, gl), lambda i, kk, te, nu: (i, 0)),
            ],
            out_specs=pl.BlockSpec((tm, d), lambda i, kk, te, nu: (i, 0)),
        ),
        out_shape=jax.ShapeDtypeStruct((p, d), _F32),
        compiler_params=_params("arbitrary", "arbitrary"),
        name="moe_down",
    )(tile_expert, n_used, h, w2, gate)


def _combine_ln_kernel(pos_ref, ys_ref, res_ref, lw_ref, lb_ref, o_ref, ob_ref, buf_ref, sem, *, alpha, tc):
    def issue(r, carry):
        for j in range(TOP_K):
            pltpu.make_async_copy(
                ys_ref.at[pl.ds(pos_ref[0, r * TOP_K + j], 1)], buf_ref.at[j, pl.ds(r, 1)], sem.at[j]).start()
        return carry

    lax.fori_loop(0, tc, issue, 0, unroll=4)
    y = alpha * res_ref[...]
    for j in range(TOP_K):
        pltpu.make_async_copy(ys_ref.at[pl.ds(0, tc)], buf_ref.at[j], sem.at[j]).wait()
        y = y + buf_ref[j]
    mu = jnp.mean(y, axis=-1, keepdims=True)
    yc = y - mu
    var = jnp.mean(yc * yc, axis=-1, keepdims=True)
    out = yc * lax.rsqrt(var + LN_EPS) * lw_ref[...] + lb_ref[...]
    o_ref[...] = out
    ob_ref[...] = out.astype(_BF16)


def _combine_ln(ys, pos, res, ln_w, ln_b, alpha, tc_target=512):
    t, d = res.shape
    tc = _tile(t, tc_target, 8)
    return pl.pallas_call(
        functools.partial(_combine_ln_kernel, alpha=alpha, tc=tc),
        grid=(t // tc,),
        in_specs=[
            pl.BlockSpec((None, 1, tc * TOP_K), lambda i: (i, 0, 0), memory_space=pltpu.SMEM),
            pl.BlockSpec(memory_space=pl.ANY),
            pl.BlockSpec((tc, d), lambda i: (i, 0)),
            pl.BlockSpec((1, d), lambda i: (0, 0)),
            pl.BlockSpec((1, d), lambda i: (0, 0)),
        ],
        out_specs=[pl.BlockSpec((tc, d), lambda i: (i, 0)), pl.BlockSpec((tc, d), lambda i: (i, 0))],
        out_shape=[jax.ShapeDtypeStruct((t, d), _F32), jax.ShapeDtypeStruct((t, d), _BF16)],
        scratch_shapes=[pltpu.VMEM((TOP_K, tc, d), _F32), pltpu.SemaphoreType.DMA((TOP_K,))],
        compiler_params=_params("arbitrary"),
        name="combine_ln",
    )(pos.reshape(t // tc, 1, tc * TOP_K), ys, res, ln_w.reshape(1, d), ln_b.reshape(1, d))


def _moe_ln(xf, xb, w_router, w1, w3, w2, ln_w, ln_b, alpha, tm_target=1024):
    t = xf.shape[0]
    n_exp = w_router.shape[1]
    tm = _tile(TOP_K * t, tm_target, 8)
    top_w, top_e = _router_top2(xf, w_router)
    src_tok, row_gate, tile_expert, n_used, pos = _route(top_w, top_e, n_exp, tm)
    xs = _gather_rows(xb, src_tok, n_used, tm)
    h = _grouped_up(xs, w1, w3, tile_expert, n_used, tm)
    ys = _grouped_down(h, w2, row_gate, tile_expert, n_used, tm)
    return _combine_ln(ys, pos, xf, ln_w, ln_b, alpha)


def _rwkv_kernel(r_ref, w_ref, k_ref, v_ref, an_ref, b_ref, y_ref, s_ref, sa_ref, *, tt, n, sub):
    @pl.when(pl.program_id(0) == 0)
    def _():
        s_ref[...] = jnp.zeros_like(s_ref)
        sa_ref[...] = jnp.zeros_like(sa_ref)

    nblk = n // sub
    zero = jnp.zeros(v_ref.shape[1:], _F32)

    def step(t, sa):
        v_t = v_ref[t]

        def upd_body(j, carry):
            y, sa_next = carry
            rows = pl.ds(pl.multiple_of(j * sub, sub), sub)
            w_blk = w_ref[t, rows, :]
            b_blk = b_ref[t, rows, :]
            k_blk = k_ref[t, rows, :]
            r_blk = r_ref[t, rows, :]
            a_blk = an_ref[t, rows, :]
            for u in range(sub):
                s_new = s_ref[j * sub + u] * w_blk[u:u + 1, :] + sa * b_blk[u:u + 1, :] + v_t * k_blk[u:u + 1, :]
                s_ref[j * sub + u] = s_new
                y = y + s_new * r_blk[u:u + 1, :]
                sa_next = sa_next + s_new * a_blk[u:u + 1, :]
            return y, sa_next

        y, sa_next = lax.fori_loop(0, nblk, upd_body, (zero, zero))
        y_ref[t] = y
        return sa_next

    sa_ref[...] = lax.fori_loop(0, tt, step, sa_ref[...])


def _rwkv_scan(r, w, k, v, a, b, tt_target=32):
    s, n, l = r.shape
    tt = _tile(s, tt_target, 1)
    sub = 8 if n % 8 == 0 else 1
    a = jnp.concatenate([a[1:], jnp.zeros_like(a[:1])], axis=0)
    spec = pl.BlockSpec((tt, n, l), lambda i: (i, 0, 0))
    return pl.pallas_call(
        functools.partial(_rwkv_kernel, tt=tt, n=n, sub=sub),
        grid=(s // tt,),
        in_specs=[spec] * 6,
        out_specs=spec,
        out_shape=jax.ShapeDtypeStruct((s, n, l), _F32),
        scratch_shapes=[pltpu.VMEM((n, n, l), _F32), pltpu.VMEM((n, l), _F32)],
        compiler_params=_params("arbitrary"),
        name="rwkv7_scan",
    )(r, w, k, v, a, b)


HGRN_CHUNK = 64
HGRN_SUB = 16


def _dot_nt(a, b):
    return lax.dot_general(a, b, (((1,), (1,)), ((), ())), preferred_element_type=_F32)


def _dot_tn(a, b):
    return lax.dot_general(a, b, (((0,), (0,)), ((), ())), preferred_element_type=_F32)


def _hgrn_chunk_kernel(q_ref, hf_ref, hi_ref, hg_ref, lb_ref, nw_ref, o_ref, st_ref, *, first_layer, chunk, sub):
    @pl.when(pl.program_id(2) == 0)
    def _():
        st_ref[...] = jnp.zeros_like(st_ref)

    ts, dk = q_ref.shape
    nsub = chunk // sub
    lb = lb_ref[...]
    tri = (lax.broadcasted_iota(jnp.int32, (chunk, chunk), 0)
           >= lax.broadcasted_iota(jnp.int32, (chunk, chunk), 1)).astype(_F32)
    local_row = lax.broadcasted_iota(jnp.int32, (chunk, 1), 0) % sub

    def rows_of_subchunks(t, s):
        return jnp.concatenate(
            [jnp.broadcast_to(t[n * sub + s:n * sub + s + 1, :], (sub, t.shape[1])) for n in range(nsub)], axis=0)

    for c in range(ts // chunk):
        rows = slice(c * chunk, (c + 1) * chunk)
        q, hf, hi = q_ref[rows, :], hf_ref[rows, :], hi_ref[rows, :]
        if first_layer:
            log_f = jax.nn.log_sigmoid(hf)
        else:
            log_f = jnp.log(lb + (1.0 - lb) * jax.nn.sigmoid(hf))
        kin = (1.0 - lb) * jax.nn.sigmoid(-hf)
        cum = jnp.dot(tri, log_f, preferred_element_type=_F32, precision=lax.Precision.HIGHEST)
        total = cum[chunk - 1:, :]
        hi_b = hi.astype(_BF16)

        st = st_ref[...]
        o = _dot_nt((q * jnp.exp(cum)).astype(_BF16), st.astype(_BF16))
        k_end = (kin * jnp.exp(total - cum)).astype(_BF16)
        st_ref[...] = st * jnp.exp(total) + _dot_tn(hi_b, k_end)

        parts = [jnp.zeros((sub, hi.shape[1]), _F32)]
        for n in range(1, nsub):
            start = cum[n * sub - 1:n * sub, :]
            q_n = q[n * sub:(n + 1) * sub, :] * jnp.exp(cum[n * sub:(n + 1) * sub, :] - start)
            k_n = kin[:n * sub, :] * jnp.exp(start - cum[:n * sub, :])
            a_n = _dot_nt(q_n.astype(_BF16), k_n.astype(_BF16))
            parts.append(jnp.dot(a_n.astype(_BF16), hi_b[:n * sub, :], preferred_element_type=_F32))
        o = o + jnp.concatenate(parts, axis=0)

        for s in range(sub):
            diff = cum - rows_of_subchunks(cum, s)
            decay = jnp.exp(jnp.where(local_row >= s, diff, -jnp.inf))
            score = jnp.sum(q * rows_of_subchunks(kin, s) * decay, axis=1, keepdims=True)
            o = o + score * rows_of_subchunks(hi, s)

        hg = hg_ref[rows, :]
        on = o * lax.rsqrt(jnp.mean(o * o, axis=-1, keepdims=True) + RMS_EPS) * nw_ref[...]
        o_ref[rows, :] = (on * hg * jax.nn.sigmoid(hg)).astype(o_ref.dtype)


def _hgrn_chunked(p, lb, norm_w, bsz, hw, h_dim, first_layer, ts_target=1024):
    m = p.shape[0]
    seq = m // bsz
    heads = hw // h_dim
    chunk = min(HGRN_CHUNK, seq)
    ts = _tile(seq, ts_target, chunk)
    nst = seq // ts
    col = lambda sec: pl.BlockSpec((ts, h_dim), lambda b, h, s: (b * nst + s, sec * heads + h))
    return pl.pallas_call(
        functools.partial(_hgrn_chunk_kernel, first_layer=first_layer, chunk=chunk, sub=min(HGRN_SUB, chunk)),
        grid=(bsz, heads, nst),
        in_specs=[col(0), col(1), col(2), col(3),
                  pl.BlockSpec((1, h_dim), lambda b, h, s: (0, h)),
                  pl.BlockSpec((1, h_dim), lambda b, h, s: (0, 0))],
        out_specs=pl.BlockSpec((ts, h_dim), lambda b, h, s: (b * nst + s, h)),
        out_shape=jax.ShapeDtypeStruct((m, hw), _BF16),
        scratch_shapes=[pltpu.VMEM((h_dim, h_dim), _F32)],
        compiler_params=_params("arbitrary", "arbitrary", "arbitrary"),
        name="hgrn2_chunked",
    )(p, p, p, p, lb.reshape(1, hw), norm_w.reshape(1, h_dim))


def _group_matrix(width, group):
    col = jnp.arange(width, dtype=jnp.int32)[:, None] // group
    return (col == jnp.arange(width // group, dtype=jnp.int32)[None, :]).astype(_BF16)


def _gsum(x, g):
    hi = x.astype(_BF16)
    lo = (x - hi.astype(_F32)).astype(_BF16)
    return jnp.dot(hi, g, preferred_element_type=_F32) + jnp.dot(lo, g, preferred_element_type=_F32)


def _prep_kernel(*refs, first_layer, ranks):
    if first_layer:
        (k_ref, t_ref, vec_ref, wt_ref, g_ref, gt_ref,
         w_out, k_out, a_out, b_out, gate_out) = refs
    else:
        (k_ref, t_ref, v_ref, vf_ref, vec_ref, wt_ref, g_ref, gt_ref,
         w_out, k_out, a_out, b_out, gate_out, v_out) = refs
    width = k_ref.shape[1]
    d_rank, a_rank, g_rank = ranks

    t = t_ref[...]
    col = lax.broadcasted_iota(jnp.int32, t.shape, 1)
    act = jnp.where(col < d_rank, jnp.tanh(t),
                    jnp.where(jnp.logical_and(col >= d_rank + a_rank, col < d_rank + a_rank + g_rank),
                              jax.nn.sigmoid(t), t))
    low = jnp.dot(act.astype(_BF16), wt_ref[...], preferred_element_type=_F32)
    w0, a0, v0, k_k, k_a = (vec_ref[n:n + 1, :] for n in range(5))
    w_log = -jax.nn.softplus(-(w0 + low[:, :width])) - 0.5
    w_out[...] = jnp.exp(-jnp.exp(w_log))
    iclr = jax.nn.sigmoid(a0 + low[:, width:2 * width])
    gate_out[...] = low[:, 2 * width:3 * width]
    if not first_layer:
        v = v_ref[...]
        v_out[...] = v + (vf_ref[...] - v) * jax.nn.sigmoid(v0 + low[:, 3 * width:])

    k = k_ref[...]
    kk = k * k_k
    inv = 1.0 / jnp.maximum(jnp.sqrt(_gsum(kk * kk, g_ref[...])), L2_EPS)
    kk = kk * _gsum(inv, gt_ref[...])
    k_out[...] = k * (1.0 + (iclr - 1.0) * k_a)
    a_out[...] = -kk
    b_out[...] = kk * iclr


def _mixer_prep(p, tail, p_first, vecs, w_tail, ranks, hw, rw, r_n, first_layer, tm_target=256):
    m = p.shape[0]
    tm = _tile(m, tm_target, 8)
    hb = 4 * hw // rw
    blk = lambda c: pl.BlockSpec((tm, rw), lambda i: (i, c))
    full = lambda a: pl.BlockSpec(a.shape, lambda i: (0,) * a.ndim)
    g = _group_matrix(rw, r_n)
    gt = g.T
    n_out = 5 if first_layer else 6
    in_specs = [blk(hb + 1), pl.BlockSpec((tm, tail.shape[1]), lambda i: (i, 0))]
    args = [p, tail]
    if not first_layer:
        in_specs += [blk(hb + 2), blk(hb + 2)]
        args += [p, p_first]
    consts = [vecs, w_tail, g, gt]
    in_specs += [full(a) for a in consts]
    return pl.pallas_call(
        functools.partial(_prep_kernel, first_layer=first_layer, ranks=ranks),
        grid=(m // tm,),
        in_specs=in_specs,
        out_specs=[pl.BlockSpec((tm, rw), lambda i: (i, 0))] * n_out,
        out_shape=[jax.ShapeDtypeStruct((m, rw), _F32)] * n_out,
        compiler_params=_params("arbitrary"),
        name="mixer_prep",
    )(*args, *consts)


def _post_kernel(y_ref, r_ref, k_ref, v_ref, gate_ref, vec_ref, gr_ref, grt_ref, o_ref, *, r_dim):
    lnx_w, lnx_b, r_k = (vec_ref[n:n + 1, :] for n in range(3))
    y = y_ref[...]
    gr = gr_ref[...]
    grt = grt_ref[...]
    yc = y - _gsum(_gsum(y, gr) * (1.0 / r_dim), grt)
    rs_y = lax.rsqrt(_gsum(yc * yc, gr) * (1.0 / r_dim) + GN_EPS)
    bonus = _gsum(_gsum(r_ref[...] * k_ref[...] * r_k, gr), grt) * v_ref[...]
    o_ref[...] = ((yc * _gsum(rs_y, grt) * lnx_w + lnx_b + bonus) * gate_ref[...]).astype(o_ref.dtype)


def _mixer_post(y, p, k, v_src, v_blk, gate, vecs, hw, rw, r_dim, tm_target=512):
    m = y.shape[0]
    tm = _tile(m, tm_target, 8)
    hb = 4 * hw // rw
    row = lambda c=0: pl.BlockSpec((tm, rw), lambda i: (i, c))
    full = lambda a: pl.BlockSpec(a.shape, lambda i: (0,) * a.ndim)
    gr = _group_matrix(rw, r_dim)
    consts = [vecs, gr, gr.T]
    return pl.pallas_call(
        functools.partial(_post_kernel, r_dim=r_dim),
        grid=(m // tm,),
        in_specs=[row(), row(hb), row(), row(v_blk), row()] + [full(a) for a in consts],
        out_specs=row(),
        out_shape=jax.ShapeDtypeStruct((m, rw), _BF16),
        compiler_params=_params("arbitrary"),
        name="mixer_post",
    )(y, p, k, v_src, gate, *consts)


def _to_lanes_kernel(x_ref, o_ref, *, ts, n):
    tiles = x_ref.shape[2] // LANES
    for s in range(ts):
        z = x_ref[:, s, :]
        zs = jnp.concatenate([z[:, j * LANES:(j + 1) * LANES] for j in range(tiles)], axis=0)
        both = jnp.concatenate([zs, pltpu.roll(zs, n, axis=1)], axis=0)
        o_ref[s] = both.T[:n, :]


def _from_lanes_kernel(y_ref, o_ref, *, ts, n):
    bsz = o_ref.shape[0]
    tiles = o_ref.shape[2] // LANES
    lane = lax.broadcasted_iota(jnp.int32, (n, LANES), 1)
    for s in range(ts):
        y = y_ref[s]
        u = jnp.concatenate([y, jnp.zeros_like(y)], axis=0).T
        zs = jnp.where(lane < n, u[:n, :], pltpu.roll(u[n:, :], n, axis=1))
        for j in range(tiles):
            o_ref[:, s, j * LANES:(j + 1) * LANES] = zs[j * bsz:(j + 1) * bsz, :]


def _fast_lanes(bsz, seq, width, n):
    return 2 * n == LANES and width % LANES == 0 and bsz * (width // LANES) * 2 == LANES and bsz % 8 == 0 and seq % 8 == 0


def _to_scan(t, bsz, heads, width=None, col=0, ts_target=32):
    m = t.shape[0]
    width = t.shape[1] if width is None else width
    seq, n = m // bsz, width // heads
    if not _fast_lanes(bsz, seq, width, n):
        t = t[:, col * width:(col + 1) * width]
        return t.reshape(bsz, seq, heads, n).transpose(1, 3, 0, 2).reshape(seq, n, bsz * heads)
    ts = _tile(seq, ts_target, 8)
    return pl.pallas_call(
        functools.partial(_to_lanes_kernel, ts=ts, n=n),
        grid=(seq // ts,),
        in_specs=[pl.BlockSpec((bsz, ts, width), lambda i: (0, i, col))],
        out_specs=pl.BlockSpec((ts, n, LANES), lambda i: (i, 0, 0)),
        out_shape=jax.ShapeDtypeStruct((seq, n, LANES), t.dtype),
        compiler_params=_params("arbitrary"),
        name="to_lanes",
    )(t.reshape(bsz, seq, t.shape[1]))


def _from_scan(t, bsz, heads, ts_target=32):
    seq, n, _ = t.shape
    width = heads * n
    if not _fast_lanes(bsz, seq, width, n):
        return t.reshape(seq, n, bsz, heads).transpose(2, 0, 3, 1).reshape(bsz * seq, width)
    ts = _tile(seq, ts_target, 8)
    return pl.pallas_call(
        functools.partial(_from_lanes_kernel, ts=ts, n=n),
        grid=(seq // ts,),
        in_specs=[pl.BlockSpec((ts, n, LANES), lambda i: (i, 0, 0))],
        out_specs=pl.BlockSpec((bsz, ts, width), lambda i: (0, i, 0)),
        out_shape=jax.ShapeDtypeStruct((bsz, seq, width), t.dtype),
        compiler_params=_params("arbitrary"),
        name="from_lanes",
    )(t).reshape(bsz * seq, width)


def kernel(x, w_in, w_in_vres, hgrn_lower_bounds, hgrn_norm_w, rwkv_mu, rwkv_mu_vres, rwkv_w0, rwkv_w2, rwkv_a0, rwkv_a2, rwkv_v0, rwkv_v2, rwkv_g2, rwkv_k_k, rwkv_k_a, rwkv_r_k, rwkv_lnx_w, rwkv_lnx_b, w_out, ln1_w, ln1_b, ffn_w1, ffn_w3, ffn_w2, moe_router, moe_w1, moe_w3, moe_w2, ln2_w, ln2_b):
    bsz, seq, d = x.shape
    depth = w_in.shape[0]
    alpha = (2 * depth) ** 0.25
    hw = hgrn_lower_bounds.shape[1]
    h_dim = hgrn_norm_w.shape[1]
    rw = rwkv_w0.shape[1]
    r_heads, r_dim = rwkv_r_k.shape[1], rwkv_r_k.shape[2]
    assert hw == rw and hw % LANES == 0
    ranks = (rwkv_w2.shape[1], rwkv_a2.shape[1], rwkv_g2.shape[1])
    v_rank = rwkv_v2.shape[1]
    main = 4 * hw + 3 * rw
    tail_w = -(-(sum(ranks) + v_rank) // LANES) * LANES
    hb = 4 * hw // rw
    m = bsz * seq

    lb_all = jax.nn.softmax(hgrn_lower_bounds.astype(_F32), axis=0)
    lb_all = jnp.cumsum(lb_all, axis=0) - lb_all[0:1]

    xf = x.reshape(m, d)
    xb = xf.astype(_BF16)
    p_first = None
    for layer in range(depth):
        first = layer == 0
        w_tail_in, mu = w_in[layer][:, main:], rwkv_mu[layer]
        if not first:
            w_tail_in = jnp.concatenate([w_tail_in, w_in_vres[layer - 1]], axis=1)
            mu = jnp.concatenate([mu, rwkv_mu_vres[layer - 1]], axis=0)
        pad = tail_w - w_tail_in.shape[1]
        w_tail_in = jnp.pad(w_tail_in, ((0, 0), (0, pad)))
        mu = jnp.concatenate([jnp.zeros((4 * hw,), _F32), mu.astype(_F32), jnp.zeros((pad,), _F32)])
        p = _in_proj(xb, w_in[layer][:, :main].astype(_BF16), mu[:main], seq, 4 * hw)
        tail = _in_proj(xb, w_tail_in.astype(_BF16), mu[main:], seq, 0)

        blocks = [rwkv_w2[layer], rwkv_a2[layer], rwkv_g2[layer]]
        if not first:
            blocks.append(rwkv_v2[layer - 1])
        w_tail_out = jnp.zeros((tail_w, 4 * rw), _F32)
        r0 = 0
        for n, blk in enumerate(blocks):
            w_tail_out = w_tail_out.at[r0:r0 + blk.shape[0], n * rw:(n + 1) * rw].set(blk)
            r0 += blk.shape[0]
        v0 = jnp.zeros((rw,), _F32) if first else rwkv_v0[layer - 1]
        vecs = jnp.stack([rwkv_w0[layer], rwkv_a0[layer], v0, rwkv_k_k[layer], rwkv_k_a[layer]]).astype(_F32)
        vecs = jnp.pad(vecs, ((0, 3), (0, 0)))
        outs = _mixer_prep(p, tail, p_first, vecs, w_tail_out.astype(_BF16), ranks, hw, rw, r_dim, first)
        if first:
            decay, k, a, b, gate = outs
            p_first = p
            v_src, v_blk = p, hb + 2
        else:
            decay, k, a, b, gate, v = outs
            v_src, v_blk = v, 0

        o_h = _hgrn_chunked(p, lb_all[layer], hgrn_norm_w[layer].astype(_F32), bsz, hw, h_dim, first)
        lanes = lambda t, col=0: _to_scan(t, bsz, r_heads, rw, col)
        y = _rwkv_scan(lanes(p, hb), lanes(decay), lanes(k), lanes(v_src, v_blk), lanes(a), lanes(b))
        y = _from_scan(y, bsz, r_heads)

        post_vecs = jnp.stack([rwkv_lnx_w[layer], rwkv_lnx_b[layer], rwkv_r_k[layer].reshape(rw)]).astype(_F32)
        post_vecs = jnp.pad(post_vecs, ((0, 5), (0, 0)))
        o_r = _mixer_post(y, p, k, v_src, v_blk, gate, post_vecs, hw, rw, r_dim)
        mix = jnp.concatenate([o_h, o_r], axis=1)
        xf, xb = _down_ln(mix, w_out[layer].astype(_BF16), xf, ln1_w[layer], ln1_b[layer], alpha)

        j = layer // 2
        if layer % 2 == 0:
            h = _swiglu_up(xb, ffn_w1[j].astype(_BF16), ffn_w3[j].astype(_BF16))
            xf, xb = _down_ln(h, ffn_w2[j].astype(_BF16), xf, ln2_w[layer], ln2_b[layer], alpha)
        else:
            xf, xb = _moe_ln(xf, xb, moe_router[j], moe_w1[j], moe_w3[j], moe_w2[j], ln2_w[layer], ln2_b[layer], alpha)
    return xf.reshape(bsz, seq, d)
```

```python
import functools
import math

import jax
import jax.numpy as jnp
from jax import lax
from jax.experimental import pallas as pl
from jax.experimental.pallas import tpu as pltpu

LN_EPS = 1e-5
RMS_EPS = 1e-5
GN_EPS = 64e-5
L2_EPS = 1e-12
TOP_K = 2

LANES = 128
VMEM_LIMIT_BYTES = 56 * 1024 * 1024

_F32 = jnp.float32
_BF16 = jnp.bfloat16


def _tile(n, target, align):
    best = None
    for t in range(align, min(n, target) + 1, align):
        if n % t == 0:
            best = t
    return best if best is not None else n


def _params(*sem):
    return pltpu.CompilerParams(dimension_semantics=sem, vmem_limit_bytes=VMEM_LIMIT_BYTES)


def _in_proj_kernel(x_ref, w_ref, mu_ref, o_ref, last_ref, *, shift_from, tiles_per_seq):
    j = pl.program_id(0)
    i = pl.program_id(1)
    acc = jnp.dot(x_ref[...], w_ref[...], preferred_element_type=_F32)

    @pl.when(j < shift_from)
    def _():
        o_ref[...] = acc

    @pl.when(j >= shift_from)
    def _():
        carry = jnp.where(i % tiles_per_seq == 0, 0.0, last_ref[...])
        row = lax.broadcasted_iota(jnp.int32, acc.shape, 0)
        prev = jnp.where(row == 0, carry, pltpu.roll(acc, 1, axis=0))
        last_ref[...] = acc[acc.shape[0] - 1:, :]
        o_ref[...] = acc + mu_ref[...] * (prev - acc)


def _in_proj(x, w, mu, seq, shift_from_col, tm_target=1024, tn_target=1024):
    m, k = x.shape
    n = w.shape[1]
    tm = _tile(seq, tm_target, 8)
    tn = _tile(math.gcd(n, shift_from_col) if shift_from_col else n, tn_target, LANES)
    return pl.pallas_call(
        functools.partial(_in_proj_kernel, shift_from=shift_from_col // tn, tiles_per_seq=seq // tm),
        grid=(n // tn, m // tm),
        in_specs=[
            pl.BlockSpec((tm, k), lambda j, i: (i, 0)),
            pl.BlockSpec((k, tn), lambda j, i: (0, j)),
            pl.BlockSpec((1, tn), lambda j, i: (0, j)),
        ],
        out_specs=pl.BlockSpec((tm, tn), lambda j, i: (i, j)),
        out_shape=jax.ShapeDtypeStruct((m, n), _F32),
        scratch_shapes=[pltpu.VMEM((1, tn), _F32)],
        compiler_params=_params("arbitrary", "arbitrary"),
        name="in_proj",
    )(x, w, mu.reshape(1, n).astype(_F32))


def _up_kernel(x_ref, w1_ref, w3_ref, o_ref):
    x = x_ref[...]
    a = jnp.dot(x, w1_ref[...], preferred_element_type=_F32)
    b = jnp.dot(x, w3_ref[...], preferred_element_type=_F32)
    o_ref[...] = (a * jax.nn.sigmoid(a) * b).astype(o_ref.dtype)


def _swiglu_up(x, w1, w3, tm_target=1024, tn_target=512):
    m, k = x.shape
    f = w1.shape[1]
    tm = _tile(m, tm_target, 8)
    tn = _tile(f, tn_target, LANES)
    w_spec = pl.BlockSpec((k, tn), lambda j, i: (0, j))
    return pl.pallas_call(
        _up_kernel,
        grid=(f // tn, m // tm),
        in_specs=[pl.BlockSpec((tm, k), lambda j, i: (i, 0)), w_spec, w_spec],
        out_specs=pl.BlockSpec((tm, tn), lambda j, i: (i, j)),
        out_shape=jax.ShapeDtypeStruct((m, f), _BF16),
        compiler_params=_params("arbitrary", "arbitrary"),
        name="swiglu_up",
    )(x, w1, w3)


def _down_ln_kernel(h_ref, w_ref, res_ref, lw_ref, lb_ref, o_ref, ob_ref, *, alpha, nk):
    kk = pl.program_id(1)
    part = jnp.dot(h_ref[...], w_ref[...], preferred_element_type=_F32)

    @pl.when(kk == 0)
    def _():
        o_ref[...] = part

    @pl.when(kk > 0)
    def _():
        o_ref[...] += part

    @pl.when(kk == nk - 1)
    def _():
        y = alpha * res_ref[...] + o_ref[...]
        mu = jnp.mean(y, axis=-1, keepdims=True)
        yc = y - mu
        var = jnp.mean(yc * yc, axis=-1, keepdims=True)
        out = yc * lax.rsqrt(var + LN_EPS) * lw_ref[...] + lb_ref[...]
        o_ref[...] = out
        ob_ref[...] = out.astype(_BF16)


def _down_ln(h, w, res, ln_w, ln_b, alpha, tm_target=512, tk_target=2048):
    m = h.shape[0]
    f, d = w.shape
    tm = _tile(m, tm_target, 8)
    tk = _tile(f, tk_target, LANES)
    nk = f // tk
    return pl.pallas_call(
        functools.partial(_down_ln_kernel, alpha=alpha, nk=nk),
        grid=(m // tm, nk),
        in_specs=[
            pl.BlockSpec((tm, tk), lambda i, kk: (i, kk)),
            pl.BlockSpec((tk, d), lambda i, kk: (kk, 0)),
            pl.BlockSpec((tm, d), lambda i, kk: (i, 0)),
            pl.BlockSpec((1, d), lambda i, kk: (0, 0)),
            pl.BlockSpec((1, d), lambda i, kk: (0, 0)),
        ],
        out_specs=[
            pl.BlockSpec((tm, d), lambda i, kk: (i, 0)),
            pl.BlockSpec((tm, d), lambda i, kk: (i, 0)),
        ],
        out_shape=[jax.ShapeDtypeStruct((m, d), _F32), jax.ShapeDtypeStruct((m, d), _BF16)],
        compiler_params=_params("arbitrary", "arbitrary"),
        name="down_ln",
    )(h, w, res, ln_w.reshape(1, d), ln_b.reshape(1, d))


def _router_kernel(x_ref, w_ref, g_ref, e_ref):
    logits = jnp.dot(x_ref[...], w_ref[...], preferred_element_type=_F32, precision=lax.Precision.HIGHEST)
    n_exp = logits.shape[1]
    col = lax.broadcasted_iota(jnp.int32, logits.shape, 1)
    m1 = jnp.max(logits, axis=1, keepdims=True)
    i1 = jnp.min(jnp.where(logits == m1, col, n_exp), axis=1, keepdims=True)
    rest = jnp.where(col == i1, -jnp.inf, logits)
    m2 = jnp.max(rest, axis=1, keepdims=True)
    i2 = jnp.min(jnp.where(rest == m2, col, n_exp), axis=1, keepdims=True)
    ex = jnp.exp(m2 - m1)
    den = 1.0 + ex
    two = lax.broadcasted_iota(jnp.int32, g_ref.shape, 1)
    g_ref[...] = jnp.where(two == 0, 1.0 / den, ex / den)
    e_ref[...] = jnp.where(two == 0, i1, i2)


def _router_top2(x, w_router, tm_target=512):
    m, d = x.shape
    e = w_router.shape[1]
    tm = _tile(m, tm_target, 8)
    return pl.pallas_call(
        _router_kernel,
        grid=(m // tm,),
        in_specs=[pl.BlockSpec((tm, d), lambda i: (i, 0)), pl.BlockSpec((d, e), lambda i: (0, 0))],
        out_specs=[pl.BlockSpec((tm, TOP_K), lambda i: (i, 0)), pl.BlockSpec((tm, TOP_K), lambda i: (i, 0))],
        out_shape=[jax.ShapeDtypeStruct((m, TOP_K), _F32), jax.ShapeDtypeStruct((m, TOP_K), jnp.int32)],
        compiler_params=_params("arbitrary"),
        name="router_top2",
    )(x, w_router)


def _route(top_w, top_e, n_exp, tm):
    t = top_e.shape[0]
    slots = TOP_K * t
    nt = slots // tm + n_exp
    flat_e = top_e.reshape(slots)
    onehot = (flat_e[:, None] == jnp.arange(n_exp, dtype=jnp.int32)[None, :]).astype(jnp.int32)
    counts = jnp.sum(onehot, axis=0)
    rank = jnp.sum((jnp.cumsum(onehot, axis=0) - onehot) * onehot, axis=1)
    padded = ((counts + tm - 1) // tm) * tm
    ends = jnp.cumsum(padded)
    starts = ends - padded
    pos = starts[flat_e] + rank
    order = jnp.argsort(pos).astype(jnp.int32)
    first_slot = jnp.cumsum(counts) - counts
    tile_expert = jnp.minimum(
        jnp.searchsorted(ends, jnp.arange(nt, dtype=jnp.int32) * tm, side="right"), n_exp - 1).astype(jnp.int32)
    row = jnp.arange(nt * tm, dtype=jnp.int32)
    row_e = tile_expert[row // tm]
    row_rank = row - starts[row_e]
    valid = row_rank < counts[row_e]
    slot = order[jnp.clip(first_slot[row_e] + row_rank, 0, slots - 1)]
    src_tok = jnp.where(valid, slot // TOP_K, 0).astype(jnp.int32)
    row_gate = jnp.where(valid, top_w.reshape(slots)[slot], 0.0)
    n_used = (ends[-1] // tm).astype(jnp.int32).reshape(1)
    return src_tok, row_gate, tile_expert, n_used, pos.astype(jnp.int32).reshape(t, TOP_K)


def _pack_halves(xb):
    half = xb.shape[1] // 2
    lo = lax.bitcast_convert_type(xb[:, :half], jnp.uint16).astype(jnp.uint32)
    hi = lax.bitcast_convert_type(xb[:, half:], jnp.uint16).astype(jnp.uint32)
    return lo | (hi << 16)


def _unpack_halves(u):
    lo = pltpu.bitcast(u << 16, _F32).astype(_BF16)
    hi = pltpu.bitcast(u & jnp.uint32(0xFFFF0000), _F32).astype(_BF16)
    return jnp.concatenate([lo, hi], axis=1)


def _gather_kernel(nu_ref, idx_ref, src_ref, o_ref, buf_ref, sem, *, tm):
    used = pl.program_id(0) < nu_ref[0]

    @pl.when(used)
    def _():
        def issue(r, carry):
            pltpu.make_async_copy(src_ref.at[pl.ds(idx_ref[0, r], 1)], buf_ref.at[pl.ds(r, 1)], sem).start()
            return carry

        lax.fori_loop(0, tm, issue, 0, unroll=8)
        pltpu.make_async_copy(src_ref.at[pl.ds(0, tm)], buf_ref, sem).wait()
        o_ref[...] = _unpack_halves(buf_ref[...])

    @pl.when(jnp.logical_not(used))
    def _():
        o_ref[...] = jnp.zeros_like(o_ref)


def _gather_rows(xb, idx, n_used, tm):
    p = idx.shape[0]
    d = xb.shape[1]
    src = _pack_halves(xb)
    return pl.pallas_call(
        functools.partial(_gather_kernel, tm=tm),
        grid_spec=pltpu.PrefetchScalarGridSpec(
            num_scalar_prefetch=1,
            grid=(p // tm,),
            in_specs=[
                pl.BlockSpec((None, 1, tm), lambda i, nu: (i, 0, 0), memory_space=pltpu.SMEM),
                pl.BlockSpec(memory_space=pl.ANY),
            ],
            out_specs=pl.BlockSpec((tm, d), lambda i, nu: (i, 0)),
            scratch_shapes=[pltpu.VMEM((tm, d // 2), jnp.uint32), pltpu.SemaphoreType.DMA(())],
        ),
        out_shape=jax.ShapeDtypeStruct((p, d), _BF16),
        compiler_params=_params("arbitrary"),
        name="gather_rows",
    )(n_used, idx.reshape(p // tm, 1, tm), src)


def _gup_kernel(te_ref, nu_ref, x_ref, w1_ref, w3_ref, o_ref, w1b_ref, w3b_ref):
    i = pl.program_id(1)
    new_expert = jnp.logical_or(i == 0, te_ref[i] != te_ref[jnp.maximum(i - 1, 0)])

    @pl.when(new_expert)
    def _():
        w1b_ref[...] = w1_ref[...].astype(_BF16)
        w3b_ref[...] = w3_ref[...].astype(_BF16)

    @pl.when(i < nu_ref[0])
    def _():
        x = x_ref[...]
        a = jnp.dot(x, w1b_ref[...], preferred_element_type=_F32)
        b = jnp.dot(x, w3b_ref[...], preferred_element_type=_F32)
        o_ref[...] = (a * jax.nn.sigmoid(a) * b).astype(o_ref.dtype)

    @pl.when(i >= nu_ref[0])
    def _():
        o_ref[...] = jnp.zeros_like(o_ref)


def _grouped_up(xs, w1, w3, tile_expert, n_used, tm, tn_target=512):
    p = xs.shape[0]
    _, d, f = w1.shape
    tn = _tile(f, tn_target, LANES)
    w_spec = pl.BlockSpec((None, d, tn), lambda j, i, te, nu: (te[i], 0, j))
    return pl.pallas_call(
        _gup_kernel,
        grid_spec=pltpu.PrefetchScalarGridSpec(
            num_scalar_prefetch=2,
            grid=(f // tn, p // tm),
            in_specs=[pl.BlockSpec((tm, xs.shape[1]), lambda j, i, te, nu: (i, 0)), w_spec, w_spec],
            out_specs=pl.BlockSpec((tm, tn), lambda j, i, te, nu: (i, j)),
            scratch_shapes=[pltpu.VMEM((d, tn), _BF16), pltpu.VMEM((d, tn), _BF16)],
        ),
        out_shape=jax.ShapeDtypeStruct((p, f), _BF16),
        compiler_params=_params("arbitrary", "arbitrary"),
        name="moe_up",
    )(tile_expert, n_used, xs, w1, w3)


def _gdown_kernel(te_ref, nu_ref, h_ref, w_ref, g_ref, o_ref, *, nk):
    i = pl.program_id(0)
    kk = pl.program_id(1)
    used = i < nu_ref[0]

    @pl.when(used)
    def _():
        part = jnp.dot(h_ref[...], w_ref[...].astype(_BF16), preferred_element_type=_F32)

        @pl.when(kk == 0)
        def _():
            o_ref[...] = part

        @pl.when(kk > 0)
        def _():
            o_ref[...] += part

        @pl.when(kk == nk - 1)
        def _():
            g = g_ref[...]
            o_ref[...] = o_ref[...] * jnp.concatenate([g] * (o_ref.shape[1] // g.shape[1]), axis=1)

    @pl.when(jnp.logical_and(jnp.logical_not(used), kk == nk - 1))
    def _():
        o_ref[...] = jnp.zeros_like(o_ref)


def _grouped_down(h, w2, row_gate, tile_expert, n_used, tm, tk_target=1024):
    p, f = h.shape
    d = w2.shape[2]
    tk = _tile(f, tk_target, LANES)
    nk = f // tk
    gl = LANES if d % LANES == 0 else d
    gate = jnp.broadcast_to(row_gate[:, None], (p, gl))
    return pl.pallas_call(
        functools.partial(_gdown_kernel, nk=nk),
        grid_spec=pltpu.PrefetchScalarGridSpec(
            num_scalar_prefetch=2,
            grid=(p // tm, nk),
            in_specs=[
                pl.BlockSpec((tm, tk), lambda i, kk, te, nu: (i, kk)),
                pl.BlockSpec((None, tk, d), lambda i, kk, te, nu: (te[i], kk, 0)),
                pl.BlockSpec((tm, gl), lambda i, kk, te, nu: (i, 0)),
            ],
            out_specs=pl.BlockSpec((tm, d), lambda i, kk, te, nu: (i, 0)),
        ),
        out_shape=jax.ShapeDtypeStruct((p, d), _F32),
        compiler_params=_params("arbitrary", "arbitrary"),
        name="moe_down",
    )(tile_expert, n_used, h, w2, gate)


def _combine_ln_kernel(pos_ref, ys_ref, res_ref, lw_ref, lb_ref, o_ref, ob_ref, buf_ref, sem, *, alpha, tc):
    def issue(r, carry):
        for j in range(TOP_K):
            pltpu.make_async_copy(
                ys_ref.at[pl.ds(pos_ref[0, r * TOP_K + j], 1)], buf_ref.at[j, pl.ds(r, 1)], sem.at[j]).start()
        return carry

    lax.fori_loop(0, tc, issue, 0, unroll=4)
    y = alpha * res_ref[...]
    for j in range(TOP_K):
        pltpu.make_async_copy(ys_ref.at[pl.ds(0, tc)], buf_ref.at[j], sem.at[j]).wait()
        y = y + buf_ref[j]
    mu = jnp.mean(y, axis=-1, keepdims=True)
    yc = y - mu
    var = jnp.mean(yc * yc, axis=-1, keepdims=True)
    out = yc * lax.rsqrt(var + LN_EPS) * lw_ref[...] + lb_ref[...]
    o_ref[...] = out
    ob_ref[...] = out.astype(_BF16)


def _combine_ln(ys, pos, res, ln_w, ln_b, alpha, tc_target=512):
    t, d = res.shape
    tc = _tile(t, tc_target, 8)
    return pl.pallas_call(
        functools.partial(_combine_ln_kernel, alpha=alpha, tc=tc),
        grid=(t // tc,),
        in_specs=[
            pl.BlockSpec((None, 1, tc * TOP_K), lambda i: (i, 0, 0), memory_space=pltpu.SMEM),
            pl.BlockSpec(memory_space=pl.ANY),
            pl.BlockSpec((tc, d), lambda i: (i, 0)),
            pl.BlockSpec((1, d), lambda i: (0, 0)),
            pl.BlockSpec((1, d), lambda i: (0, 0)),
        ],
        out_specs=[pl.BlockSpec((tc, d), lambda i: (i, 0)), pl.BlockSpec((tc, d), lambda i: (i, 0))],
        out_shape=[jax.ShapeDtypeStruct((t, d), _F32), jax.ShapeDtypeStruct((t, d), _BF16)],
        scratch_shapes=[pltpu.VMEM((TOP_K, tc, d), _F32), pltpu.SemaphoreType.DMA((TOP_K,))],
        compiler_params=_params("arbitrary"),
        name="combine_ln",
    )(pos.reshape(t // tc, 1, tc * TOP_K), ys, res, ln_w.reshape(1, d), ln_b.reshape(1, d))


def _moe_ln(xf, xb, w_router, w1, w3, w2, ln_w, ln_b, alpha, tm_target=1024):
    t = xf.shape[0]
    n_exp = w_router.shape[1]
    tm = _tile(TOP_K * t, tm_target, 8)
    top_w, top_e = _router_top2(xf, w_router)
    src_tok, row_gate, tile_expert, n_used, pos = _route(top_w, top_e, n_exp, tm)
    xs = _gather_rows(xb, src_tok, n_used, tm)
    h = _grouped_up(xs, w1, w3, tile_expert, n_used, tm)
    ys = _grouped_down(h, w2, row_gate, tile_expert, n_used, tm)
    return _combine_ln(ys, pos, xf, ln_w, ln_b, alpha)


def _rwkv_kernel(r_ref, w_ref, k_ref, v_ref, an_ref, b_ref, y_ref, s_ref, sa_ref, *, tt, n, sub):
    @pl.when(pl.program_id(0) == 0)
    def _():
        s_ref[...] = jnp.zeros_like(s_ref)
        sa_ref[...] = jnp.zeros_like(sa_ref)

    nblk = n // sub
    zero = jnp.zeros(v_ref.shape[1:], _F32)

    def step(t, sa):
        v_t = v_ref[t]

        def upd_body(j, carry):
            y, sa_next = carry
            rows = pl.ds(pl.multiple_of(j * sub, sub), sub)
            w_blk = w_ref[t, rows, :]
            b_blk = b_ref[t, rows, :]
            k_blk = k_ref[t, rows, :]
            r_blk = r_ref[t, rows, :]
            a_blk = an_ref[t, rows, :]
            for u in range(sub):
                s_new = s_ref[j * sub + u] * w_blk[u:u + 1, :] + sa * b_blk[u:u + 1, :] + v_t * k_blk[u:u + 1, :]
                s_ref[j * sub + u] = s_new
                y = y + s_new * r_blk[u:u + 1, :]
                sa_next = sa_next + s_new * a_blk[u:u + 1, :]
            return y, sa_next

        y, sa_next = lax.fori_loop(0, nblk, upd_body, (zero, zero))
        y_ref[t] = y
        return sa_next

    sa_ref[...] = lax.fori_loop(0, tt, step, sa_ref[...])


def _rwkv_scan(r, w, k, v, a, b, tt_target=32):
    s, n, l = r.shape
    tt = _tile(s, tt_target, 1)
    sub = 8 if n % 8 == 0 else 1
    a = jnp.concatenate([a[1:], jnp.zeros_like(a[:1])], axis=0)
    spec = pl.BlockSpec((tt, n, l), lambda i: (i, 0, 0))
    return pl.pallas_call(
        functools.partial(_rwkv_kernel, tt=tt, n=n, sub=sub),
        grid=(s // tt,),
        in_specs=[spec] * 6,
        out_specs=spec,
        out_shape=jax.ShapeDtypeStruct((s, n, l), _F32),
        scratch_shapes=[pltpu.VMEM((n, n, l), _F32), pltpu.VMEM((n, l), _F32)],
        compiler_params=_params("arbitrary"),
        name="rwkv7_scan",
    )(r, w, k, v, a, b)


HGRN_CHUNK = 64
HGRN_SUB = 16


def _dot_nt(a, b):
    return lax.dot_general(a, b, (((1,), (1,)), ((), ())), preferred_element_type=_F32)


def _dot_tn(a, b):
    return lax.dot_general(a, b, (((0,), (0,)), ((), ())), preferred_element_type=_F32)


def _hgrn_chunk_kernel(q_ref, hf_ref, hi_ref, hg_ref, lb_ref, nw_ref, o_ref, st_ref, *, first_layer, chunk, sub):
    @pl.when(pl.program_id(2) == 0)
    def _():
        st_ref[...] = jnp.zeros_like(st_ref)

    ts, dk = q_ref.shape
    nsub = chunk // sub
    lb = lb_ref[...]
    tri = (lax.broadcasted_iota(jnp.int32, (chunk, chunk), 0)
           >= lax.broadcasted_iota(jnp.int32, (chunk, chunk), 1)).astype(_F32)
    local_row = lax.broadcasted_iota(jnp.int32, (chunk, 1), 0) % sub

    def rows_of_subchunks(t, s):
        return jnp.concatenate(
            [jnp.broadcast_to(t[n * sub + s:n * sub + s + 1, :], (sub, t.shape[1])) for n in range(nsub)], axis=0)

    for c in range(ts // chunk):
        rows = slice(c * chunk, (c + 1) * chunk)
        q, hf, hi = q_ref[rows, :], hf_ref[rows, :], hi_ref[rows, :]
        if first_layer:
            log_f = jax.nn.log_sigmoid(hf)
        else:
            log_f = jnp.log(lb + (1.0 - lb) * jax.nn.sigmoid(hf))
        kin = (1.0 - lb) * jax.nn.sigmoid(-hf)
        cum = jnp.dot(tri, log_f, preferred_element_type=_F32, precision=lax.Precision.HIGHEST)
        total = cum[chunk - 1:, :]
        hi_b = hi.astype(_BF16)

        st = st_ref[...]
        o = _dot_nt((q * jnp.exp(cum)).astype(_BF16), st.astype(_BF16))
        k_end = (kin * jnp.exp(total - cum)).astype(_BF16)
        st_ref[...] = st * jnp.exp(total) + _dot_tn(hi_b, k_end)

        parts = [jnp.zeros((sub, hi.shape[1]), _F32)]
        for n in range(1, nsub):
            start = cum[n * sub - 1:n * sub, :]
            q_n = q[n * sub:(n + 1) * sub, :] * jnp.exp(cum[n * sub:(n + 1) * sub, :] - start)
            k_n = kin[:n * sub, :] * jnp.exp(start - cum[:n * sub, :])
            a_n = _dot_nt(q_n.astype(_BF16), k_n.astype(_BF16))
            parts.append(jnp.dot(a_n.astype(_BF16), hi_b[:n * sub, :], preferred_element_type=_F32))
        o = o + jnp.concatenate(parts, axis=0)

        for s in range(sub):
            diff = cum - rows_of_subchunks(cum, s)
            decay = jnp.exp(jnp.where(local_row >= s, diff, -jnp.inf))
            score = jnp.sum(q * rows_of_subchunks(kin, s) * decay, axis=1, keepdims=True)
            o = o + score * rows_of_subchunks(hi, s)

        hg = hg_ref[rows, :]
        on = o * lax.rsqrt(jnp.mean(o * o, axis=-1, keepdims=True) + RMS_EPS) * nw_ref[...]
        o_ref[rows, :] = (on * hg * jax.nn.sigmoid(hg)).astype(o_ref.dtype)


def _hgrn_chunked(p, lb, norm_w, bsz, hw, h_dim, first_layer, ts_target=1024):
    m = p.shape[0]
    seq = m // bsz
    heads = hw // h_dim
    chunk = min(HGRN_CHUNK, seq)
    ts = _tile(seq, ts_target, chunk)
    nst = seq // ts
    col = lambda sec: pl.BlockSpec((ts, h_dim), lambda b, h, s: (b * nst + s, sec * heads + h))
    return pl.pallas_call(
        functools.partial(_hgrn_chunk_kernel, first_layer=first_layer, chunk=chunk, sub=min(HGRN_SUB, chunk)),
        grid=(bsz, heads, nst),
        in_specs=[col(0), col(1), col(2), col(3),
                  pl.BlockSpec((1, h_dim), lambda b, h, s: (0, h)),
                  pl.BlockSpec((1, h_dim), lambda b, h, s: (0, 0))],
        out_specs=pl.BlockSpec((ts, h_dim), lambda b, h, s: (b * nst + s, h)),
        out_shape=jax.ShapeDtypeStruct((m, hw), _BF16),
        scratch_shapes=[pltpu.VMEM((h_dim, h_dim), _F32)],
        compiler_params=_params("arbitrary", "arbitrary", "arbitrary"),
        name="hgrn2_chunked",
    )(p, p, p, p, lb.reshape(1, hw), norm_w.reshape(1, h_dim))


def _group_matrix(width, group):
    col = jnp.arange(width, dtype=jnp.int32)[:, None] // group
    return (col == jnp.arange(width // group, dtype=jnp.int32)[None, :]).astype(_BF16)


def _gsum(x, g):
    hi = x.astype(_BF16)
    lo = (x - hi.astype(_F32)).astype(_BF16)
    return jnp.dot(hi, g, preferred_element_type=_F32) + jnp.dot(lo, g, preferred_element_type=_F32)


def _prep_kernel(*refs, first_layer, ranks):
    if first_layer:
        (k_ref, t_ref, vec_ref, wt_ref, g_ref, gt_ref,
         w_out, k_out, a_out, b_out, gate_out) = refs
    else:
        (k_ref, t_ref, v_ref, vf_ref, vec_ref, wt_ref, g_ref, gt_ref,
         w_out, k_out, a_out, b_out, gate_out, v_out) = refs
    width = k_ref.shape[1]
    d_rank, a_rank, g_rank = ranks

    t = t_ref[...]
    col = lax.broadcasted_iota(jnp.int32, t.shape, 1)
    act = jnp.where(col < d_rank, jnp.tanh(t),
                    jnp.where(jnp.logical_and(col >= d_rank + a_rank, col < d_rank + a_rank + g_rank),
                              jax.nn.sigmoid(t), t))
    low = jnp.dot(act.astype(_BF16), wt_ref[...], preferred_element_type=_F32)
    w0, a0, v0, k_k, k_a = (vec_ref[n:n + 1, :] for n in range(5))
    w_log = -jax.nn.softplus(-(w0 + low[:, :width])) - 0.5
    w_out[...] = jnp.exp(-jnp.exp(w_log))
    iclr = jax.nn.sigmoid(a0 + low[:, width:2 * width])
    gate_out[...] = low[:, 2 * width:3 * width]
    if not first_layer:
        v = v_ref[...]
        v_out[...] = v + (vf_ref[...] - v) * jax.nn.sigmoid(v0 + low[:, 3 * width:])

    k = k_ref[...]
    kk = k * k_k
    inv = 1.0 / jnp.maximum(jnp.sqrt(_gsum(kk * kk, g_ref[...])), L2_EPS)
    kk = kk * _gsum(inv, gt_ref[...])
    k_out[...] = k * (1.0 + (iclr - 1.0) * k_a)
    a_out[...] = -kk
    b_out[...] = kk * iclr


def _mixer_prep(p, tail, p_first, vecs, w_tail, ranks, hw, rw, r_n, first_layer, tm_target=256):
    m = p.shape[0]
    tm = _tile(m, tm_target, 8)
    hb = 4 * hw // rw
    blk = lambda c: pl.BlockSpec((tm, rw), lambda i: (i, c))
    full = lambda a: pl.BlockSpec(a.shape, lambda i: (0,) * a.ndim)
    g = _group_matrix(rw, r_n)
    gt = g.T
    n_out = 5 if first_layer else 6
    in_specs = [blk(hb + 1), pl.BlockSpec((tm, tail.shape[1]), lambda i: (i, 0))]
    args = [p, tail]
    if not first_layer:
        in_specs += [blk(hb + 2), blk(hb + 2)]
        args += [p, p_first]
    consts = [vecs, w_tail, g, gt]
    in_specs += [full(a) for a in consts]
    return pl.pallas_call(
        functools.partial(_prep_kernel, first_layer=first_layer, ranks=ranks),
        grid=(m // tm,),
        in_specs=in_specs,
        out_specs=[pl.BlockSpec((tm, rw), lambda i: (i, 0))] * n_out,
        out_shape=[jax.ShapeDtypeStruct((m, rw), _F32)] * n_out,
        compiler_params=_params("arbitrary"),
        name="mixer_prep",
    )(*args, *consts)


def _post_kernel(y_ref, r_ref, k_ref, v_ref, gate_ref, vec_ref, gr_ref, grt_ref, o_ref, *, r_dim):
    lnx_w, lnx_b, r_k = (vec_ref[n:n + 1, :] for n in range(3))
    y = y_ref[...]
    gr = gr_ref[...]
    grt = grt_ref[...]
    yc = y - _gsum(_gsum(y, gr) * (1.0 / r_dim), grt)
    rs_y = lax.rsqrt(_gsum(yc * yc, gr) * (1.0 / r_dim) + GN_EPS)
    bonus = _gsum(_gsum(r_ref[...] * k_ref[...] * r_k, gr), grt) * v_ref[...]
    o_ref[...] = ((yc * _gsum(rs_y, grt) * lnx_w + lnx_b + bonus) * gate_ref[...]).astype(o_ref.dtype)


def _mixer_post(y, p, k, v_src, v_blk, gate, vecs, hw, rw, r_dim, tm_target=512):
    m = y.shape[0]
    tm = _tile(m, tm_target, 8)
    hb = 4 * hw // rw
    row = lambda c=0: pl.BlockSpec((tm, rw), lambda i: (i, c))
    full = lambda a: pl.BlockSpec(a.shape, lambda i: (0,) * a.ndim)
    gr = _group_matrix(rw, r_dim)
    consts = [vecs, gr, gr.T]
    return pl.pallas_call(
        functools.partial(_post_kernel, r_dim=r_dim),
        grid=(m // tm,),
        in_specs=[row(), row(hb), row(), row(v_blk), row()] + [full(a) for a in consts],
        out_specs=row(),
        out_shape=jax.ShapeDtypeStruct((m, rw), _BF16),
        compiler_params=_params("arbitrary"),
        name="mixer_post",
    )(y, p, k, v_src, gate, *consts)


def _to_lanes_kernel(*refs, ts, n):
    count = len(refs) // 2
    for x_ref, o_ref in zip(refs[:count], refs[count:]):
        tiles = x_ref.shape[2] // LANES
        for s in range(ts):
            z = x_ref[:, s, :]
            zs = jnp.concatenate([z[:, j * LANES:(j + 1) * LANES] for j in range(tiles)], axis=0)
            zt = zs.T
            o_ref[s] = jnp.concatenate([zt[:n, :], zt[n:, :]], axis=1)


def _from_lanes_kernel(y_ref, o_ref, *, ts, n):
    bsz = o_ref.shape[0]
    tiles = o_ref.shape[2] // LANES
    lane = lax.broadcasted_iota(jnp.int32, (n, LANES), 1)
    for s in range(ts):
        y = y_ref[s]
        u = jnp.concatenate([y, jnp.zeros_like(y)], axis=0).T
        zs = jnp.where(lane < n, u[:n, :], pltpu.roll(u[n:, :], n, axis=1))
        for j in range(tiles):
            o_ref[:, s, j * LANES:(j + 1) * LANES] = zs[j * bsz:(j + 1) * bsz, :]


def _fast_lanes(bsz, seq, width, n):
    return 2 * n == LANES and width % LANES == 0 and bsz * (width // LANES) * 2 == LANES and bsz % 8 == 0 and seq % 8 == 0


def _to_scan(items, bsz, heads, width, ts_target=32):
    m = items[0][0].shape[0]
    seq, n = m // bsz, width // heads
    if not _fast_lanes(bsz, seq, width, n):
        return [t[:, col * width:(col + 1) * width].reshape(bsz, seq, heads, n).transpose(1, 3, 0, 2)
                .reshape(seq, n, bsz * heads) for t, col in items]
    ts = _tile(seq, ts_target, 8)
    in_spec = lambda col: pl.BlockSpec((bsz, ts, width), lambda i: (0, i, col))
    return pl.pallas_call(
        functools.partial(_to_lanes_kernel, ts=ts, n=n),
        grid=(seq // ts,),
        in_specs=[in_spec(col) for _, col in items],
        out_specs=[pl.BlockSpec((ts, n, LANES), lambda i: (i, 0, 0))] * len(items),
        out_shape=[jax.ShapeDtypeStruct((seq, n, LANES), t.dtype) for t, _ in items],
        compiler_params=_params("arbitrary"),
        name="to_lanes",
    )(*[t.reshape(bsz, seq, t.shape[1]) for t, _ in items])


def _from_scan(t, bsz, heads, ts_target=32):
    seq, n, _ = t.shape
    width = heads * n
    if not _fast_lanes(bsz, seq, width, n):
        return t.reshape(seq, n, bsz, heads).transpose(2, 0, 3, 1).reshape(bsz * seq, width)
    ts = _tile(seq, ts_target, 8)
    return pl.pallas_call(
        functools.partial(_from_lanes_kernel, ts=ts, n=n),
        grid=(seq // ts,),
        in_specs=[pl.BlockSpec((ts, n, LANES), lambda i: (i, 0, 0))],
        out_specs=pl.BlockSpec((bsz, ts, width), lambda i: (0, i, 0)),
        out_shape=jax.ShapeDtypeStruct((bsz, seq, width), t.dtype),
        compiler_params=_params("arbitrary"),
        name="from_lanes",
    )(t).reshape(bsz * seq, width)


def kernel(x, w_in, w_in_vres, hgrn_lower_bounds, hgrn_norm_w, rwkv_mu, rwkv_mu_vres, rwkv_w0, rwkv_w2, rwkv_a0, rwkv_a2, rwkv_v0, rwkv_v2, rwkv_g2, rwkv_k_k, rwkv_k_a, rwkv_r_k, rwkv_lnx_w, rwkv_lnx_b, w_out, ln1_w, ln1_b, ffn_w1, ffn_w3, ffn_w2, moe_router, moe_w1, moe_w3, moe_w2, ln2_w, ln2_b):
    bsz, seq, d = x.shape
    depth = w_in.shape[0]
    alpha = (2 * depth) ** 0.25
    hw = hgrn_lower_bounds.shape[1]
    h_dim = hgrn_norm_w.shape[1]
    rw = rwkv_w0.shape[1]
    r_heads, r_dim = rwkv_r_k.shape[1], rwkv_r_k.shape[2]
    assert hw == rw and hw % LANES == 0
    ranks = (rwkv_w2.shape[1], rwkv_a2.shape[1], rwkv_g2.shape[1])
    v_rank = rwkv_v2.shape[1]
    main = 4 * hw + 3 * rw
    tail_w = -(-(sum(ranks) + v_rank) // LANES) * LANES
    hb = 4 * hw // rw
    m = bsz * seq

    lb_all = jax.nn.softmax(hgrn_lower_bounds.astype(_F32), axis=0)
    lb_all = jnp.cumsum(lb_all, axis=0) - lb_all[0:1]

    xf = x.reshape(m, d)
    xb = xf.astype(_BF16)
    p_first = None
    for layer in range(depth):
        first = layer == 0
        w_tail_in, mu = w_in[layer][:, main:], rwkv_mu[layer]
        if not first:
            w_tail_in = jnp.concatenate([w_tail_in, w_in_vres[layer - 1]], axis=1)
            mu = jnp.concatenate([mu, rwkv_mu_vres[layer - 1]], axis=0)
        pad = tail_w - w_tail_in.shape[1]
        w_tail_in = jnp.pad(w_tail_in, ((0, 0), (0, pad)))
        mu = jnp.concatenate([jnp.zeros((4 * hw,), _F32), mu.astype(_F32), jnp.zeros((pad,), _F32)])
        p = _in_proj(xb, w_in[layer][:, :main].astype(_BF16), mu[:main], seq, 4 * hw)
        tail = _in_proj(xb, w_tail_in.astype(_BF16), mu[main:], seq, 0)

        blocks = [rwkv_w2[layer], rwkv_a2[layer], rwkv_g2[layer]]
        if not first:
            blocks.append(rwkv_v2[layer - 1])
        w_tail_out = jnp.zeros((tail_w, 4 * rw), _F32)
        r0 = 0
        for n, blk in enumerate(blocks):
            w_tail_out = w_tail_out.at[r0:r0 + blk.shape[0], n * rw:(n + 1) * rw].set(blk)
            r0 += blk.shape[0]
        v0 = jnp.zeros((rw,), _F32) if first else rwkv_v0[layer - 1]
        vecs = jnp.stack([rwkv_w0[layer], rwkv_a0[layer], v0, rwkv_k_k[layer], rwkv_k_a[layer]]).astype(_F32)
        vecs = jnp.pad(vecs, ((0, 3), (0, 0)))
        outs = _mixer_prep(p, tail, p_first, vecs, w_tail_out.astype(_BF16), ranks, hw, rw, r_dim, first)
        if first:
            decay, k, a, b, gate = outs
            p_first = p
            v_src, v_blk = p, hb + 2
        else:
            decay, k, a, b, gate, v = outs
            v_src, v_blk = v, 0

        o_h = _hgrn_chunked(p, lb_all[layer], hgrn_norm_w[layer].astype(_F32), bsz, hw, h_dim, first)
        scan_in = _to_scan([(p, hb), (decay, 0), (k, 0), (v_src, v_blk), (a, 0), (b, 0)], bsz, r_heads, rw)
        y = _from_scan(_rwkv_scan(*scan_in), bsz, r_heads)

        post_vecs = jnp.stack([rwkv_lnx_w[layer], rwkv_lnx_b[layer], rwkv_r_k[layer].reshape(rw)]).astype(_F32)
        post_vecs = jnp.pad(post_vecs, ((0, 5), (0, 0)))
        o_r = _mixer_post(y, p, k, v_src, v_blk, gate, post_vecs, hw, rw, r_dim)
        mix = jnp.concatenate([o_h, o_r], axis=1)
        xf, xb = _down_ln(mix, w_out[layer].astype(_BF16), xf, ln1_w[layer], ln1_b[layer], alpha)

        j = layer // 2
        if layer % 2 == 0:
            h = _swiglu_up(xb, ffn_w1[j].astype(_BF16), ffn_w3[j].astype(_BF16))
            xf, xb = _down_ln(h, ffn_w2[j].astype(_BF16), xf, ln2_w[layer], ln2_b[layer], alpha)
        else:
            xf, xb = _moe_ln(xf, xb, moe_router[j], moe_w1[j], moe_w3[j], moe_w2[j], ln2_w[layer], ln2_b[layer], alpha)
    return xf.reshape(bsz, seq, d)
```
